```python
import jax, jax.numpy as jnp
from jax import lax
import numpy as np

D_MODEL = 1024
BATCH = 4
SEQ = 4096
DEPTH = 2

GRID_W = 64
CTX_LEN = 256

DN_HEADS = 4
DN_DK = 128
DN_DV = 128
DN_WIDTH = DN_HEADS * DN_DV
CONV_K = 5
CHUNK = 64

MLA_HEADS = 8
Q_RANK = 384
KV_RANK = 256
NOPE_DIM = 64
ROPE_DIM = 32
V_DIM = 64
MLA_WIDTH = MLA_HEADS * V_DIM
QK_SCALE = (NOPE_DIM + ROPE_DIM) ** -0.5
ROPE_BASE = 10000.0
Q_BLOCK = 128

F_GROUPS = 4
F_GROUP_DIM = 128
F_WIDTH = F_GROUPS * F_GROUP_DIM

N_BRANCH = 3
BRANCH_WIDTH = 512

D_FF = 2816
N_EXPERTS = 8
TOP_K = 2
EXPERT_FF = 3584

RMS_EPS = 1e-6

IN_NAMES = ("dn_q", "dn_k", "dn_v", "dn_z", "dn_a", "dn_b", "mla_cq", "mla_ckv", "mla_kpe", "fourier", "gates")
IN_SIZES = (DN_WIDTH, DN_WIDTH, DN_WIDTH, DN_WIDTH, 2 * DN_HEADS, 2 * DN_HEADS, Q_RANK, KV_RANK, ROPE_DIM, F_WIDTH, N_BRANCH * D_MODEL)
IN_WIDTH = sum(IN_SIZES)

kernel_name = "hybrid_deltanet_mla_fourier_moe_dit"

F32 = jnp.float32


def rmsnorm(x, g):
    xf = x.astype(F32)
    y = xf * lax.rsqrt(jnp.mean(xf * xf, -1, keepdims=True) + RMS_EPS)
    return (y * g.astype(F32)).astype(x.dtype)


def l2norm(x):
    xf = x.astype(F32)
    return xf * lax.rsqrt(jnp.sum(xf * xf, -1, keepdims=True) + RMS_EPS)


def modulate(h, shift, scale):
    return h * (1 + scale) + shift


def split_cols(u):
    out, off = {}, 0
    for name, size in zip(IN_NAMES, IN_SIZES):
        out[name] = u[..., off:off + size]
        off += size
    return out


def centred_dwconv(x, w):
    k = w.shape[0]
    return lax.conv_general_dilated(x, w[:, None, :].astype(x.dtype), window_strides=(1,),
                                    padding=[(k // 2, k // 2)],
                                    dimension_numbers=("NWC", "WIO", "NWC"),
                                    feature_group_count=x.shape[-1])


def axial_rope_cs(n_tokens):
    rows = n_tokens // GRID_W
    r, col = jnp.meshgrid(jnp.arange(rows, dtype=F32), jnp.arange(GRID_W, dtype=F32), indexing="ij")
    axis_dim = ROPE_DIM // 2
    inv = ROPE_BASE ** (-jnp.arange(0, axis_dim, 2, dtype=F32) / axis_dim)
    ang_r = r.reshape(-1, 1, 1) * inv
    ang_c = col.reshape(-1, 1, 1) * inv
    return (jnp.cos(ang_r), jnp.sin(ang_r), jnp.cos(ang_c), jnp.sin(ang_c))


def rope_half(x, cos, sin):
    x1, x2 = jnp.split(x, 2, -1)
    return jnp.concatenate([x1 * cos - x2 * sin, x2 * cos + x1 * sin], -1)


def axial_rope(x, cs):
    cr, sr, cc, sc = cs
    xr, xcol = jnp.split(x.astype(F32), 2, -1)
    return jnp.concatenate([rope_half(xr, cr, sr), rope_half(xcol, cc, sc)], -1).astype(x.dtype)


def gated_delta_chunked(q, k, v, g, beta, s0, with_output):
    B, T, H, dk = k.shape
    dv = v.shape[-1]
    n = T // CHUNK

    def chunks(t):
        t = t.astype(F32).reshape(B, n, CHUNK, H, *t.shape[3:])
        return jnp.moveaxis(t, (1, 3), (0, 2))

    q, k, v, g, beta = (chunks(t) for t in (q, k, v, g, beta))
    gc = jnp.cumsum(g, -1)
    idx = jnp.arange(CHUNK)
    incl = idx[:, None] >= idx[None, :]
    strict = idx[:, None] > idx[None, :]
    decay = jnp.exp(jnp.where(incl, gc[..., :, None] - gc[..., None, :], -jnp.inf))
    kb = k * beta[..., None]
    a_mat = jnp.where(strict, jnp.einsum("nbhik,nbhjk->nbhij", kb, k) * decay, 0.0)
    eye = jnp.eye(CHUNK, dtype=F32)
    tmat = lax.linalg.triangular_solve(a_mat + eye, jnp.broadcast_to(eye, a_mat.shape),
                                       left_side=True, lower=True, unit_diagonal=True)
    u = tmat @ (v * beta[..., None])
    w = tmat @ (kb * jnp.exp(gc)[..., None])
    g_last = gc[..., -1]
    kd = k * jnp.exp(g_last[..., None] - gc)[..., None]
    gl = jnp.exp(g_last)
    if with_output:
        qs = q * (dk ** -0.5)
        attn = jnp.einsum("nbhik,nbhjk->nbhij", qs, k) * decay
        qg = qs * jnp.exp(gc)[..., None]
        xs = (u, w, kd, gl, qg, attn)
    else:
        xs = (u, w, kd, gl)

    def step(S, xs_i):
        u_i, w_i, kd_i, gl_i = xs_i[:4]
        v_new = u_i - jnp.einsum("bhck,bhkv->bhcv", w_i, S)
        S_next = S * gl_i[..., None, None] + jnp.einsum("bhck,bhcv->bhkv", kd_i, v_new)
        if with_output:
            qg_i, attn_i = xs_i[4:]
            o = jnp.einsum("bhck,bhkv->bhcv", qg_i, S) + jnp.einsum("bhcd,bhdv->bhcv", attn_i, v_new)
            return S_next, o
        return S_next, None

    s_fin, o = lax.scan(step, s0.astype(F32), xs)
    if not with_output:
        return None, s_fin
    o = jnp.moveaxis(o, (0, 2), (1, 3)).reshape(B, T, H, dv)
    return o, s_fin


def dn_inputs(s, conv_w, a_log, dt_bias):
    qkv = jax.nn.silu(centred_dwconv(jnp.concatenate([s["dn_q"], s["dn_k"], s["dn_v"]], -1), conv_w))
    B, T, _ = qkv.shape
    qkv = qkv.reshape(B, T, 3, DN_HEADS, DN_DK)
    q, k, v = l2norm(qkv[:, :, 0]), l2norm(qkv[:, :, 1]), qkv[:, :, 2]
    a = s["dn_a"].astype(F32).reshape(B, T, 2, DN_HEADS)
    b = s["dn_b"].astype(F32).reshape(B, T, 2, DN_HEADS)
    g = -jnp.exp(a_log.astype(F32)) * jax.nn.softplus(a + dt_bias.astype(F32))
    return q, k, v, g, jax.nn.sigmoid(b)


def flip_dir(t, d):
    return jnp.flip(t, 1) if d else t


def dn_out(o, z, gain):
    B, T = z.shape[:2]
    o = o.astype(z.dtype)
    y = rmsnorm(o, gain) * jax.nn.silu(z.reshape(B, T, DN_HEADS, DN_DV))
    return y.reshape(B, T, DN_WIDTH)


def mla_q(s, p, rope_cs):
    B, T, _ = s["mla_cq"].shape
    q = (rmsnorm(s["mla_cq"], p["mla_q_norm"]) @ p["mla_w_qup"]).reshape(B, T, MLA_HEADS, NOPE_DIM + ROPE_DIM)
    q_pe = q[..., NOPE_DIM:]
    if rope_cs is not None:
        q_pe = axial_rope(q_pe, rope_cs)
    return jnp.concatenate([q[..., :NOPE_DIM], q_pe], -1)


def mla_kv(s, p, rope_cs):
    B, T, _ = s["mla_ckv"].shape
    kv = (rmsnorm(s["mla_ckv"], p["mla_kv_norm"]) @ p["mla_w_kvup"]).reshape(B, T, MLA_HEADS, NOPE_DIM + V_DIM)
    k_pe = s["mla_kpe"][:, :, None, :]
    if rope_cs is not None:
        k_pe = axial_rope(k_pe, rope_cs)
    k = jnp.concatenate([kv[..., :NOPE_DIM], jnp.broadcast_to(k_pe, (B, T, MLA_HEADS, ROPE_DIM))], -1)
    return k, kv[..., NOPE_DIM:]


def block_attention(q, k, v):
    B, Tq, H, d = q.shape
    nb = Tq // Q_BLOCK
    qb = jnp.swapaxes(q.reshape(B, nb, Q_BLOCK, H, d), 0, 1)

    def one(qi):
        s = jnp.einsum("bqhd,bkhd->bhqk", qi, k).astype(F32) * QK_SCALE
        pr = jax.nn.softmax(s, -1).astype(v.dtype)
        return jnp.einsum("bhqk,bkhv->bqhv", pr, v)

    o = lax.map(one, qb)
    return jnp.swapaxes(o, 0, 1).reshape(B, Tq, H * v.shape[-1])


def fourier_mix(u):
    B, T, _ = u.shape
    f = jnp.fft.fft2(u.reshape(B, T, F_GROUPS, F_GROUP_DIM).astype(F32), axes=(1, 3), norm="ortho").real
    return f.reshape(B, T, F_WIDTH).astype(u.dtype)


def merge_branches(branches, gates_raw, w_branch, w_out):
    y = jnp.stack(branches, -2)
    yd = jnp.einsum("btnc,ncd->btnd", y, w_branch)
    g = jax.nn.sigmoid(gates_raw.reshape(*gates_raw.shape[:-1], N_BRANCH, D_MODEL))
    return jnp.sum(g * yd, -2) @ w_out


def token_mixer(hx, hc, p, rope_cs, need_ctx):
    sx = split_cols(hx @ p["w_in"])
    sc = split_cols(hc @ p["w_in"])
    dx = dn_inputs(sx, p["dn_conv"], p["dn_a_log"], p["dn_dt_bias"])
    dc = dn_inputs(sc, p["dn_conv"], p["dn_a_log"], p["dn_dt_bias"])
    s0 = jnp.zeros((hx.shape[0], DN_HEADS, DN_DK, DN_DV), F32)
    ox_dirs, oc_dirs = [], []
    for d in range(2):
        o_c, s_c = gated_delta_chunked(flip_dir(dc[0], d), flip_dir(dc[1], d), flip_dir(dc[2], d),
                                       flip_dir(dc[3][:, :, d], d), flip_dir(dc[4][:, :, d], d), s0, need_ctx)
        o_x, _ = gated_delta_chunked(flip_dir(dx[0], d), flip_dir(dx[1], d), flip_dir(dx[2], d),
                                     flip_dir(dx[3][:, :, d], d), flip_dir(dx[4][:, :, d], d), s_c, True)
        ox_dirs.append(flip_dir(o_x, d))
        if need_ctx:
            oc_dirs.append(flip_dir(o_c, d))
    ya_x = dn_out(ox_dirs[0] + ox_dirs[1], sx["dn_z"], p["dn_norm"])
    kx, vx = mla_kv(sx, p, rope_cs)
    kc, vc = mla_kv(sc, p, None)
    yb_x = block_attention(mla_q(sx, p, rope_cs), jnp.concatenate([kx, kc], 1), jnp.concatenate([vx, vc], 1))
    yc_x = fourier_mix(sx["fourier"])
    out_x = merge_branches([ya_x, yb_x, yc_x], sx["gates"], p["w_branch"], p["w_out"])
    if not need_ctx:
        return out_x, None
    ya_c = dn_out(oc_dirs[0] + oc_dirs[1], sc["dn_z"], p["dn_norm"])
    yb_c = block_attention(mla_q(sc, p, None), kc, vc)
    yc_c = fourier_mix(sc["fourier"])
    out_c = merge_branches([ya_c, yb_c, yc_c], sc["gates"], p["w_branch"], p["w_out"])
    return out_x, out_c


def swiglu(h, wg, wu, wd):
    return (jax.nn.silu(h @ wg) * (h @ wu)) @ wd


def moe_swiglu(h, w_router, wg, wu, wd):
    logits = (h @ w_router).astype(F32)
    top_v, top_i = lax.top_k(logits, TOP_K)
    wts = jax.nn.softmax(top_v, -1)
    combine = jnp.sum(jax.nn.one_hot(top_i, N_EXPERTS, dtype=F32) * wts[..., None], -2).astype(h.dtype)
    out = jnp.zeros(h.shape[:-1] + (wd.shape[-1],), h.dtype)
    for e in range(N_EXPERTS):
        out = out + combine[..., e:e + 1] * swiglu(h, wg[e], wu[e], wd[e])
    return out


def setup_inputs(seed: int = 0) -> dict:
    key = jax.random.key(seed)
    keys = iter(jax.random.split(key, 40))

    def nrm(shape, scale):
        return jax.random.normal(next(keys), shape, F32) * scale

    L = DEPTH
    n_dense = (DEPTH + 1) // 2
    n_moe = DEPTH // 2
    D = D_MODEL
    return {
        "x": nrm((BATCH, SEQ, D), 1.0),
        "c": nrm((BATCH, D), 1.0),
        "ctx": nrm((BATCH, CTX_LEN, D), 1.0),
        "c_ctx": nrm((D,), 1.0),
        "w_mod": nrm((L, D, 6 * D), 0.5 * D ** -0.5),
        "b_mod": nrm((L, 6 * D), 0.02),
        "norm_mix": 1.0 + nrm((L, D), 0.02),
        "norm_ffn": 1.0 + nrm((L, D), 0.02),
        "w_in": nrm((L, D, IN_WIDTH), D ** -0.5),
        "dn_conv": nrm((L, CONV_K, 3 * DN_WIDTH), CONV_K ** -0.5),
        "dn_a_log": jnp.log(jax.random.uniform(next(keys), (L, 2, DN_HEADS), F32, 1.0, 16.0)),
        "dn_dt_bias": 1.0 + nrm((L, 2, DN_HEADS), 0.1),
        "dn_norm": 1.0 + nrm((L, DN_DV), 0.02),
        "mla_q_norm": 1.0 + nrm((L, Q_RANK), 0.02),
        "mla_kv_norm": 1.0 + nrm((L, KV_RANK), 0.02),
        "mla_w_qup": nrm((L, Q_RANK, MLA_HEADS * (NOPE_DIM + ROPE_DIM)), Q_RANK ** -0.5),
        "mla_w_kvup": nrm((L, KV_RANK, MLA_HEADS * (NOPE_DIM + V_DIM)), KV_RANK ** -0.5),
        "w_branch": nrm((L, N_BRANCH, BRANCH_WIDTH, D), BRANCH_WIDTH ** -0.5),
        "w_out": nrm((L, D, D), D ** -0.5),
        "ffn_w_gate": nrm((n_dense, D, D_FF), D ** -0.5),
        "ffn_w_up": nrm((n_dense, D, D_FF), D ** -0.5),
        "ffn_w_down": nrm((n_dense, D_FF, D), D_FF ** -0.5),
        "moe_router": nrm((n_moe, D, N_EXPERTS), D ** -0.5),
        "moe_w_gate": nrm((n_moe, N_EXPERTS, D, EXPERT_FF), D ** -0.5),
        "moe_w_up": nrm((n_moe, N_EXPERTS, D, EXPERT_FF), D ** -0.5),
        "moe_w_down": nrm((n_moe, N_EXPERTS, EXPERT_FF, D), EXPERT_FF ** -0.5),
        "final_norm": 1.0 + nrm((D,), 0.02),
    }


def reference(x, c, ctx, c_ctx, w_mod, b_mod, norm_mix, norm_ffn, w_in, dn_conv, dn_a_log, dn_dt_bias,
              dn_norm, mla_q_norm, mla_kv_norm, mla_w_qup, mla_w_kvup, w_branch, w_out,
              ffn_w_gate, ffn_w_up, ffn_w_down, moe_router, moe_w_gate, moe_w_up, moe_w_down, final_norm):
    rope_cs = axial_rope_cs(x.shape[1])
    xc = ctx
    c_act = jax.nn.silu(c)
    cc_act = jax.nn.silu(c_ctx)
    for l in range(DEPTH):
        last = l == DEPTH - 1
        mod_x = jnp.split((c_act @ w_mod[l] + b_mod[l])[:, None, :], 6, -1)
        mod_c = jnp.split(cc_act @ w_mod[l] + b_mod[l], 6, -1)
        p = {
            "w_in": w_in[l], "dn_conv": dn_conv[l], "dn_a_log": dn_a_log[l], "dn_dt_bias": dn_dt_bias[l],
            "dn_norm": dn_norm[l], "mla_q_norm": mla_q_norm[l], "mla_kv_norm": mla_kv_norm[l],
            "mla_w_qup": mla_w_qup[l], "mla_w_kvup": mla_w_kvup[l], "w_branch": w_branch[l], "w_out": w_out[l],
        }
        hx = modulate(rmsnorm(x, norm_mix[l]), mod_x[0], mod_x[1])
        hc = modulate(rmsnorm(xc, norm_mix[l]), mod_c[0], mod_c[1])
        mix_x, mix_c = token_mixer(hx, hc, p, rope_cs, not last)
        x = x + mod_x[2] * mix_x
        if l % 2 == 0:
            e = l // 2
            wts = (ffn_w_gate[e], ffn_w_up[e], ffn_w_down[e])
            channel_mixer = swiglu
        else:
            e = l // 2
            wts = (moe_router[e], moe_w_gate[e], moe_w_up[e], moe_w_down[e])
            channel_mixer = moe_swiglu
        hx = modulate(rmsnorm(x, norm_ffn[l]), mod_x[3], mod_x[4])
        x = x + mod_x[5] * channel_mixer(hx, *wts)
        if not last:
            xc = xc + mod_c[2] * mix_c
            hc = modulate(rmsnorm(xc, norm_ffn[l]), mod_c[3], mod_c[4])
            xc = xc + mod_c[5] * channel_mixer(hc, *wts)
    return rmsnorm(x, final_norm)
```

```python
import functools

import numpy as np
import jax
import jax.numpy as jnp
from jax import lax
from jax.experimental import pallas as pl
from jax.experimental.pallas import tpu as pltpu

F32 = jnp.float32
BF16 = jnp.bfloat16

D = 1024
B = 4
T = 4096
CTX = 256
DEPTH = 2
GRID_W = 64
H_DN = 4
DK = 128
DN_W = H_DN * DK
CONV_K = 5
CHUNK = 64
H_MLA = 8
Q_RANK = 384
KV_RANK = 256
NOPE = 64
ROPE = 32
V_DIM = 64
QK_SCALE = (NOPE + ROPE) ** -0.5
ROPE_BASE = 10000.0
F_GROUPS = 4
F_GDIM = 128
F_W = F_GROUPS * F_GDIM
D_FF = 2816
N_EXP = 8
TOP_K = 2
E_FF = 3584
EPS = 1e-6

MX = B * T
MC = B * CTX
M = MX + MC

HP = 128
C_GATES = 0
C_QKV = 3 * D
C_Z = C_QKV + 3 * DN_W
C_F = C_Z + DN_W
C_CQ = C_F + F_W
CQ_PAD = 512
C_CKV = C_CQ + CQ_PAD
N_MAIN = C_CKV + KV_RANK
N_SMALL = 256

V7X_VMEM_BYTES = 64 * 1024 * 1024
V7X_VMEM_USABLE = 56 * 1024 * 1024


def _cparams(sem, vmem_est_bytes):
    limit = int(min(V7X_VMEM_USABLE, max(32 * 1024 * 1024, 2 * vmem_est_bytes)))
    return pltpu.CompilerParams(dimension_semantics=sem, vmem_limit_bytes=limit)


def _norm_mod(x, gain, shift, scale):
    y = x * lax.rsqrt(jnp.mean(x * x, -1, keepdims=True) + EPS) * gain
    return y * (1.0 + scale) + shift


def _bdot(a, b):
    return jnp.dot(a, b, preferred_element_type=F32)


def _split3(v):
    hi = v.astype(BF16)
    r1 = v - hi.astype(F32)
    mid = r1.astype(BF16)
    lo = (r1 - mid.astype(F32)).astype(BF16)
    return hi, mid, lo


def _mod_kernel(a_ref, w_ref, b_ref, o_ref):
    a = a_ref[...]
    a = a * jax.nn.sigmoid(a)
    w = w_ref[...]
    a_hi = a.astype(BF16)
    a_lo = (a - a_hi.astype(F32)).astype(BF16)
    w_hi = w.astype(BF16)
    w_lo = (w - w_hi.astype(F32)).astype(BF16)
    o_ref[...] = _bdot(a_hi, w_hi) + _bdot(a_lo, w_hi) + _bdot(a_hi, w_lo) + b_ref[...]


def _mod_vectors(c, c_ctx, w_mod, b_mod):
    tn = 1536
    a = jnp.concatenate([c, c_ctx[None, :], jnp.zeros((3, D), F32)], 0)
    out = pl.pallas_call(
        _mod_kernel,
        grid=(DEPTH, 6 * D // tn),
        in_specs=[
            pl.BlockSpec((8, D), lambda l, j: (0, 0)),
            pl.BlockSpec((None, D, tn), lambda l, j: (l, 0, j)),
            pl.BlockSpec((None, 1, tn), lambda l, j: (l, 0, j)),
        ],
        out_specs=pl.BlockSpec((None, 8, tn), lambda l, j: (l, 0, j)),
        out_shape=jax.ShapeDtypeStruct((DEPTH, 8, 6 * D), F32),
        compiler_params=_cparams(("arbitrary", "arbitrary"), 2 * D * tn * 4 * 2),
        name="mod_vectors",
    )(a, w_mod, b_mod.reshape(DEPTH, 1, 6 * D))
    return out.reshape(DEPTH, 8, 6, D)[:, : B + 1]


def _mod_spec(tm):
    tpb = T // tm
    return pl.BlockSpec((None, 6, D), lambda i, *_: (jnp.minimum(i // tpb, B), 0, 0))


def _inproj_kernel(x_ref, g_ref, mod_ref, w_ref, ws_ref, o_ref, os_ref, h_ref):
    @pl.when(pl.program_id(1) == 0)
    def _():
        hb = _norm_mod(x_ref[...], g_ref[...], mod_ref[0:1, :], mod_ref[1:2, :]).astype(BF16)
        h_ref[...] = hb
        os_ref[...] = _bdot(hb, ws_ref[...])

    o_ref[...] = _bdot(h_ref[...], w_ref[...]).astype(o_ref.dtype)


def _inproj(X, gain, mod, w_main, w_small):
    tm, tn = 512, 1280
    est = 2 * (tm * D * 4 + D * tn * 2 + D * N_SMALL * 2 + tm * tn * 2 + tm * N_SMALL * 4) + tm * D * 2
    return pl.pallas_call(
        _inproj_kernel,
        grid=(M // tm, N_MAIN // tn),
        in_specs=[
            pl.BlockSpec((tm, D), lambda i, j: (i, 0)),
            pl.BlockSpec((1, D), lambda i, j: (0, 0)),
            _mod_spec(tm),
            pl.BlockSpec((D, tn), lambda i, j: (0, j)),
            pl.BlockSpec((D, N_SMALL), lambda i, j: (0, 0)),
        ],
        out_specs=[
            pl.BlockSpec((tm, tn), lambda i, j: (i, j)),
            pl.BlockSpec((tm, N_SMALL), lambda i, j: (i, 0)),
        ],
        out_shape=[jax.ShapeDtypeStruct((M, N_MAIN), BF16), jax.ShapeDtypeStruct((M, N_SMALL), F32)],
        scratch_shapes=[pltpu.VMEM((tm, D), BF16)],
        compiler_params=_cparams(("arbitrary", "arbitrary"), est),
        name="in_proj",
    )(X, gain, mod, w_main, w_small)


DN_TT = 256
DN_HALO = 16


def _dnpre_kernel(cur_ref, prev_ref, next_ref, sm_ref, cw_ref, al_ref, dtb_ref, qkv_ref, gb_ref):
    i = pl.program_id(0)
    tiles_per_seq = T // DN_TT
    is_ctx = i >= MX // DN_TT
    first = jnp.logical_or(is_ctx, i % tiles_per_seq == 0)
    last = jnp.logical_or(is_ctx, i % tiles_per_seq == tiles_per_seq - 1)
    prev = jnp.where(first, 0.0, prev_ref[...].astype(F32))
    nxt = jnp.where(last, 0.0, next_ref[...].astype(F32))
    xc = jnp.concatenate([prev, cur_ref[...].astype(F32), nxt], axis=0)
    n = xc.shape[0]
    half = CONV_K // 2
    acc = xc * cw_ref[half:half + 1, :]
    for j in range(CONV_K):
        if j != half:
            acc = acc + pltpu.roll(xc, (half - j) % n, 0) * cw_ref[j:j + 1, :]
    y = acc[DN_HALO:DN_HALO + DN_TT]
    y = y * jax.nn.sigmoid(y)
    for p in range(3 * H_DN):
        blk = y[:, p * DK:(p + 1) * DK]
        if p < 2 * H_DN:
            blk = blk * lax.rsqrt(jnp.sum(blk * blk, -1, keepdims=True) + EPS)
        qkv_ref[:, p * DK:(p + 1) * DK] = blk.astype(BF16)

    sm = sm_ref[...]
    lane = lax.broadcasted_iota(jnp.int32, sm.shape, 1)
    g = -jnp.exp(al_ref[...]) * jax.nn.softplus(sm + dtb_ref[...])
    gb_ref[...] = jnp.where(lane < 2 * H_DN, g, jax.nn.sigmoid(sm))


def _dnpre(u, usm, conv_w, a_log, dt_bias):
    tt = DN_TT
    nb16 = tt // DN_HALO
    qkv_cb = C_QKV // (3 * DN_W)
    pad = jnp.zeros((1, 128 - 2 * H_DN), F32)
    al = jnp.concatenate([a_log.reshape(1, 2 * H_DN), pad], 1)
    dtb = jnp.concatenate([dt_bias.reshape(1, 2 * H_DN), pad], 1)
    est = 2 * (tt * 1536 * 2 * 2 + tt * 128 * 4 * 2) + 6 * (tt + 32) * 1536 * 4
    return pl.pallas_call(
        _dnpre_kernel,
        grid=(M // tt,),
        in_specs=[
            pl.BlockSpec((tt, 3 * DN_W), lambda i: (i, qkv_cb)),
            pl.BlockSpec((DN_HALO, 3 * DN_W), lambda i: (jnp.maximum(i * nb16 - 1, 0), qkv_cb)),
            pl.BlockSpec((DN_HALO, 3 * DN_W), lambda i: (jnp.minimum((i + 1) * nb16, M // DN_HALO - 1), qkv_cb)),
            pl.BlockSpec((tt, 128), lambda i: (i, 0)),
            pl.BlockSpec((CONV_K, 3 * DN_W), lambda i: (0, 0)),
            pl.BlockSpec((1, 128), lambda i: (0, 0)),
            pl.BlockSpec((1, 128), lambda i: (0, 0)),
        ],
        out_specs=[
            pl.BlockSpec((tt, 3 * DN_W), lambda i: (i, 0)),
            pl.BlockSpec((tt, 128), lambda i: (i, 0)),
        ],
        out_shape=[jax.ShapeDtypeStruct((M, 3 * DN_W), BF16), jax.ShapeDtypeStruct((M, 128), F32)],
        compiler_params=_cparams(("arbitrary",), est),
        name="dn_pre",
    )(u, u, u, usm, conv_w, al, dtb)


N_XCH = T // CHUNK
N_CCH = CTX // CHUNK
N_STEPS = N_XCH + N_CCH


def _dnscan_kernel(qkv0_ref, qkv1_ref, gb0_ref, gb1_ref, o0_ref, o1_ref, s_ref):
    @pl.when(pl.program_id(1) == 0)
    def _():
        s_ref[...] = jnp.zeros_like(s_ref)

    ii = lax.broadcasted_iota(jnp.int32, (CHUNK, CHUNK), 0)
    jj = lax.broadcasted_iota(jnp.int32, (CHUNK, CHUNK), 1)
    eye = (ii == jj).astype(F32)
    qscale = DK ** -0.5
    for d in range(2):
        qkv_ref = (qkv0_ref, qkv1_ref)[d]
        o_ref = (o0_ref, o1_ref)[d]
        gbv = (gb0_ref, gb1_ref)[d][...]
        incl = (ii >= jj) if d == 0 else (ii <= jj)
        strict = (ii > jj) if d == 0 else (ii < jj)
        tri = incl.astype(BF16)
        g_hi, g_mid, g_lo = _split3(gbv)
        gc = _bdot(tri, g_hi) + _bdot(tri, g_mid) + _bdot(tri, g_lo)
        gct = jnp.concatenate([gc, jnp.zeros_like(gc)], 0).T
        last_row = CHUNK - 1 if d == 0 else 0
        for h in range(H_DN):
            l = d * H_DN + h
            col = gc[:, l:l + 1]
            row = gct[l:l + 1, 0:CHUNK]
            glast = gc[last_row:last_row + 1, l:l + 1]
            beta = gbv[:, 2 * H_DN + l:2 * H_DN + l + 1]
            dec = jnp.exp(jnp.where(incl, col - row, -jnp.inf))
            q = qkv_ref[:, h * DK:(h + 1) * DK]
            k = qkv_ref[:, DN_W + h * DK:DN_W + (h + 1) * DK]
            v = qkv_ref[:, 2 * DN_W + h * DK:2 * DN_W + (h + 1) * DK]
            kf = k.astype(F32)
            kb = kf * beta
            qs = q.astype(F32) * qscale
            ecol = jnp.exp(col)
            lhs = jnp.concatenate([kb.astype(BF16), qs.astype(BF16)], 0)
            kq = lax.dot_general(lhs, k, (((1,), (1,)), ((), ())), preferred_element_type=F32)
            nmat = -jnp.where(strict, kq[0:CHUNK] * dec, 0.0)
            attn = kq[CHUNK:2 * CHUNK] * dec
            xinv = eye + nmat
            pw = _bdot(nmat.astype(BF16), nmat.astype(BF16))
            for it in range(5):
                pwb = pw.astype(BF16)
                xinv = xinv + _bdot(xinv.astype(BF16), pwb)
                if it < 4:
                    pw = _bdot(pwb, pwb)
            rhs = jnp.concatenate([(v.astype(F32) * beta).astype(BF16), (kb * ecol).astype(BF16)], 1)
            uw = _bdot(xinv.astype(BF16), rhs)
            u = uw[:, 0:DK]
            w = uw[:, DK:2 * DK]
            s_old = s_ref[l]
            sb = s_old.astype(BF16)
            wq = jnp.concatenate([w.astype(BF16), (qs * ecol).astype(BF16)], 0)
            wqs = _bdot(wq, sb)
            v_new = u - wqs[0:CHUNK]
            vb = v_new.astype(BF16)
            o = wqs[CHUNK:2 * CHUNK] + _bdot(attn.astype(BF16), vb)
            kd = (kf * jnp.exp(glast - col)).astype(BF16)
            s_ref[l] = s_old * jnp.exp(glast) + lax.dot_general(
                kd, vb, (((0,), (0,)), ((), ())), preferred_element_type=F32)
            o_ref[:, h * DK:(h + 1) * DK] = o.astype(o_ref.dtype)


def _dnscan(qkv, gb):
    xblk = MX // CHUNK

    def idx0(b, s):
        return jnp.where(s < N_CCH, xblk + b * N_CCH + s, b * N_XCH + s - N_CCH)

    def idx1(b, s):
        return jnp.where(s < N_CCH, xblk + b * N_CCH + (N_CCH - 1 - s), b * N_XCH + (N_STEPS - 1 - s))

    est = 2 * 2 * (CHUNK * 1536 * 2 + CHUNK * 128 * 4 + CHUNK * 512 * 4) + 8 * DK * DK * 4 + (4 << 20)
    return pl.pallas_call(
        _dnscan_kernel,
        grid=(B, N_STEPS),
        in_specs=[
            pl.BlockSpec((CHUNK, 3 * DN_W), lambda b, s: (idx0(b, s), 0)),
            pl.BlockSpec((CHUNK, 3 * DN_W), lambda b, s: (idx1(b, s), 0)),
            pl.BlockSpec((CHUNK, 128), lambda b, s: (idx0(b, s), 0)),
            pl.BlockSpec((CHUNK, 128), lambda b, s: (idx1(b, s), 0)),
        ],
        out_specs=[
            pl.BlockSpec((CHUNK, DN_W), lambda b, s: (idx0(b, s), 0)),
            pl.BlockSpec((CHUNK, DN_W), lambda b, s: (idx1(b, s), 0)),
        ],
        out_shape=[jax.ShapeDtypeStruct((M, DN_W), F32), jax.ShapeDtypeStruct((M, DN_W), F32)],
        scratch_shapes=[pltpu.VMEM((2 * H_DN, DK, DK), F32)],
        compiler_params=_cparams(("arbitrary", "arbitrary"), est),
        name="dn_scan",
    )(qkv, qkv, gb, gb)


def _mlaproj_kernel(cq_ref, ckv_ref, sm_ref, cos_ref, sin_ref, gq_ref, gkv_ref,
                    wq1_ref, wq2_ref, wk_ref, wv_ref, q_ref, k_ref, v_ref):
    cq = cq_ref[...].astype(F32)
    hq = (cq * lax.rsqrt(jnp.sum(cq * cq, -1, keepdims=True) * (1.0 / Q_RANK) + EPS) * gq_ref[...]).astype(BF16)
    ckv = ckv_ref[...].astype(F32)
    hkv = (ckv * lax.rsqrt(jnp.mean(ckv * ckv, -1, keepdims=True) + EPS) * gkv_ref[...]).astype(BF16)
    q1 = _bdot(hq, wq1_ref[...])
    q2 = _bdot(hq, wq2_ref[...])
    kn = _bdot(hkv, wk_ref[...])
    vv = _bdot(hkv, wv_ref[...])
    cosq = cos_ref[...]
    sin = sin_ref[...]
    lane = lax.broadcasted_iota(jnp.int32, cosq.shape, 1)
    cosk = jnp.where(lane >= NOPE, cosq, 0.0)
    kpe = sm_ref[:, 0:HP] * cosk + sm_ref[:, HP:2 * HP] * sin
    for h in range(H_MLA):
        sl = slice(h * HP, (h + 1) * HP)
        q_ref[:, sl] = ((q1[:, sl] * cosq + q2[:, sl] * sin) * QK_SCALE).astype(BF16)
        k_ref[:, sl] = (kn[:, sl] + kpe).astype(BF16)
        v_ref[:, sl] = jnp.where(lane == V_DIM, 1.0, vv[:, sl]).astype(BF16)


def _mlaproj(u, usm, cos_t, sin_t, gq, gkv, wq1, wq2, wk, wv):
    tm = 256
    wide = H_MLA * HP
    est = 2 * (tm * (CQ_PAD + KV_RANK) * 2 + tm * 256 * 4 + 2 * tm * 128 * 4 + 2 * CQ_PAD * wide * 2
               + 2 * KV_RANK * wide * 2 + 3 * tm * wide * 2) + 6 * tm * wide * 4
    full = lambda shape: pl.BlockSpec(shape, lambda i: (0, 0))
    return pl.pallas_call(
        _mlaproj_kernel,
        grid=(M // tm,),
        in_specs=[
            pl.BlockSpec((tm, CQ_PAD), lambda i: (i, C_CQ // CQ_PAD)),
            pl.BlockSpec((tm, KV_RANK), lambda i: (i, C_CKV // KV_RANK)),
            pl.BlockSpec((tm, N_SMALL), lambda i: (i, 0)),
            pl.BlockSpec((tm, HP), lambda i: (i, 0)),
            pl.BlockSpec((tm, HP), lambda i: (i, 0)),
            full((1, CQ_PAD)), full((1, KV_RANK)),
            full((CQ_PAD, wide)), full((CQ_PAD, wide)), full((KV_RANK, wide)), full((KV_RANK, wide)),
        ],
        out_specs=[pl.BlockSpec((tm, wide), lambda i: (i, 0))] * 3,
        out_shape=[jax.ShapeDtypeStruct((M, wide), BF16)] * 3,
        compiler_params=_cparams(("arbitrary",), est),
        name="mla_proj",
    )(u, u, usm, cos_t, sin_t, gq, gkv, wq1, wq2, wk, wv)


ATT_TQ = 256


def _attn_kernel(q_ref, kx_ref, vx_ref, kc_ref, vc_ref, o_ref):
    qi = pl.program_id(2)
    q = q_ref[...]
    nt = (((1,), (1,)), ((), ()))
    sc = lax.dot_general(q, kc_ref[...], nt, preferred_element_type=F32)
    mc = jnp.max(sc, -1, keepdims=True)

    @pl.when(qi < T // ATT_TQ)
    def _():
        sx = lax.dot_general(q, kx_ref[...], nt, preferred_element_type=F32)
        m = jnp.maximum(mc, jnp.max(sx, -1, keepdims=True))
        acc = _bdot(jnp.exp(sx - m).astype(BF16), vx_ref[...])
        acc = acc + _bdot(jnp.exp(sc - m).astype(BF16), vc_ref[...])
        o_ref[...] = (acc / acc[:, V_DIM:V_DIM + 1]).astype(o_ref.dtype)

    @pl.when(qi >= T // ATT_TQ)
    def _():
        acc = _bdot(jnp.exp(sc - mc).astype(BF16), vc_ref[...])
        o_ref[...] = (acc / acc[:, V_DIM:V_DIM + 1]).astype(o_ref.dtype)


def _attention(q, k, v, with_ctx_queries):
    tq = ATT_TQ
    nqx = T // tq
    nq = nqx + (1 if with_ctx_queries else 0)
    cblk = MX // CTX
    rows = M if with_ctx_queries else MX

    def qidx(b, h, i):
        return (jnp.where(i < nqx, b * nqx + i, MX // tq + b), h)

    est = 2 * (2 * T * HP * 2 + 2 * CTX * HP * 2 + 2 * tq * HP * 2) + 3 * tq * (T + CTX) * 4
    return pl.pallas_call(
        _attn_kernel,
        grid=(B, H_MLA, nq),
        in_specs=[
            pl.BlockSpec((tq, HP), qidx),
            pl.BlockSpec((T, HP), lambda b, h, i: (b, h)),
            pl.BlockSpec((T, HP), lambda b, h, i: (b, h)),
            pl.BlockSpec((CTX, HP), lambda b, h, i: (cblk + b, h)),
            pl.BlockSpec((CTX, HP), lambda b, h, i: (cblk + b, h)),
        ],
        out_specs=pl.BlockSpec((tq, HP), qidx),
        out_shape=jax.ShapeDtypeStruct((rows, H_MLA * HP), BF16),
        compiler_params=_cparams(("arbitrary", "arbitrary", "arbitrary"), est),
        name="mla_attention",
    )(q, k, v, k, v)


def _mm1_kernel(a_ref, b_ref, o_ref):
    o_ref[...] = _bdot(a_ref[...], b_ref[...]).astype(o_ref.dtype)


def _fourier_channels(u, w_ab):
    tm = 512
    est = 2 * (tm * F_W * 2 + F_W * 2 * F_W * 2 + tm * 2 * F_W * 2)
    return pl.pallas_call(
        _mm1_kernel,
        grid=(M // tm,),
        in_specs=[pl.BlockSpec((tm, F_W), lambda i: (i, C_F // F_W)),
                  pl.BlockSpec((F_W, 2 * F_W), lambda i: (0, 0))],
        out_specs=pl.BlockSpec((tm, 2 * F_W), lambda i: (i, 0)),
        out_shape=jax.ShapeDtypeStruct((M, 2 * F_W), BF16),
        compiler_params=_cparams(("arbitrary",), est),
        name="fourier_channels",
    )(u, w_ab)


def _dft_kernel(cs_ref, ab_ref, *rest, nk):
    o_ref, acc_ref = rest[-2], rest[-1]
    kk = pl.program_id(2)

    @pl.when(kk == 0)
    def _():
        acc_ref[...] = jnp.zeros_like(acc_ref)

    acc_ref[...] += _bdot(cs_ref[...], ab_ref[...])

    @pl.when(kk == nk - 1)
    def _():
        o_ref[...] = acc_ref[...].astype(o_ref.dtype)


def _fourier_tokens(cs, ab, seq, row0, tm, tk, prev=None):
    nk = 2 * seq // tk
    kpb = seq // tk
    in_specs = [
        pl.BlockSpec((tm, tk), lambda b, i, kk: (i, kk)),
        pl.BlockSpec((tk, F_W), lambda b, i, kk: (row0 // tk + b * kpb + kk % kpb, kk // kpb)),
    ]
    args = [cs, ab]
    aliases = {}
    if prev is not None:
        in_specs.append(pl.BlockSpec(memory_space=pl.ANY))
        args.append(prev)
        aliases = {2: 0}
    est = 2 * (tm * tk * 2 + tk * F_W * 2 + tm * F_W * 2) + tm * F_W * 4
    return pl.pallas_call(
        functools.partial(_dft_kernel, nk=nk),
        grid=(B, seq // tm, nk),
        in_specs=in_specs,
        out_specs=pl.BlockSpec((tm, F_W), lambda b, i, kk: (row0 // tm + b * (seq // tm) + i, 0)),
        out_shape=jax.ShapeDtypeStruct((M, F_W), BF16),
        scratch_shapes=[pltpu.VMEM((tm, F_W), F32)],
        input_output_aliases=aliases,
        compiler_params=_cparams(("arbitrary", "arbitrary", "arbitrary"), est),
        name="fourier_tokens",
    )(*args)


def _merge_kernel(x_ref, mod_ref, o0_ref, o1_ref, z_ref, yb_ref, yc_ref, gt_ref, gdn_ref,
                  wa_ref, wb_ref, wc_ref, wo_ref, out_ref):
    o = o0_ref[...] + o1_ref[...]
    z = z_ref[...].astype(F32)
    gdn = gdn_ref[...]
    ya = []
    for h in range(H_DN):
        sl = slice(h * DK, (h + 1) * DK)
        oh = o[:, sl]
        zh = z[:, sl]
        yh = oh * lax.rsqrt(jnp.mean(oh * oh, -1, keepdims=True) + EPS) * gdn
        ya.append((yh * (zh * jax.nn.sigmoid(zh))).astype(BF16))
    ya = jnp.concatenate(ya, -1)
    s = jax.nn.sigmoid(gt_ref[:, 0:D].astype(F32)) * _bdot(ya, wa_ref[...])
    s = s + jax.nn.sigmoid(gt_ref[:, D:2 * D].astype(F32)) * _bdot(yb_ref[...], wb_ref[...])
    s = s + jax.nn.sigmoid(gt_ref[:, 2 * D:3 * D].astype(F32)) * _bdot(yc_ref[...], wc_ref[...])
    mix = _bdot(s.astype(BF16), wo_ref[...])
    out_ref[...] = x_ref[...] + mod_ref[2:3, :] * mix


def _merge(X, mod, o0, o1, u, yb, yc, gdn, wa, wb, wc, wo, rows):
    tm = 256
    full = lambda shape: pl.BlockSpec(shape, lambda i: (0, 0))
    est = 2 * (2 * tm * D * 4 + 2 * tm * DN_W * 4 + tm * DN_W * 2 * 2 + tm * D * 2 + tm * 3 * D * 2
               + (2 * DN_W + 2 * D) * D * 2) + 6 * tm * D * 4
    return pl.pallas_call(
        _merge_kernel,
        grid=(rows // tm,),
        in_specs=[
            pl.BlockSpec((tm, D), lambda i: (i, 0)),
            _mod_spec(tm),
            pl.BlockSpec((tm, DN_W), lambda i: (i, 0)),
            pl.BlockSpec((tm, DN_W), lambda i: (i, 0)),
            pl.BlockSpec((tm, DN_W), lambda i: (i, C_Z // DN_W)),
            pl.BlockSpec((tm, H_MLA * HP), lambda i: (i, 0)),
            pl.BlockSpec((tm, F_W), lambda i: (i, 0)),
            pl.BlockSpec((tm, 3 * D), lambda i: (i, 0)),
            full((1, DK)),
            full((DN_W, D)), full((H_MLA * HP, D)), full((F_W, D)), full((D, D)),
        ],
        out_specs=pl.BlockSpec((tm, D), lambda i: (i, 0)),
        out_shape=jax.ShapeDtypeStruct((rows, D), F32),
        compiler_params=_cparams(("arbitrary",), est),
        name="merge",
    )(X, mod, o0, o1, u, yb, yc, u, gdn, wa, wb, wc, wo)


def _ffn_kernel(x_ref, g_ref, mod_ref, wg_ref, wu_ref, wd_ref, o_ref, h_ref, acc_ref, *, nf):
    f = pl.program_id(1)

    @pl.when(f == 0)
    def _():
        h_ref[...] = _norm_mod(x_ref[...], g_ref[...], mod_ref[3:4, :], mod_ref[4:5, :]).astype(BF16)
        acc_ref[...] = jnp.zeros_like(acc_ref)

    h = h_ref[...]
    a = _bdot(h, wg_ref[...])
    up = _bdot(h, wu_ref[...])
    act = (a * jax.nn.sigmoid(a) * up).astype(BF16)
    acc_ref[...] += _bdot(act, wd_ref[...])

    @pl.when(f == nf - 1)
    def _():
        o_ref[...] = x_ref[...] + mod_ref[5:6, :] * acc_ref[...]


def _ffn(X, gain, mod, wg, wu, wd):
    tm, tf = 512, 1408
    nf = D_FF // tf
    est = 2 * (2 * tm * D * 4 + 3 * D * tf * 2) + tm * D * 6 + 3 * tm * tf * 4
    return pl.pallas_call(
        functools.partial(_ffn_kernel, nf=nf),
        grid=(M // tm, nf),
        in_specs=[
            pl.BlockSpec((tm, D), lambda i, f: (i, 0)),
            pl.BlockSpec((1, D), lambda i, f: (0, 0)),
            _mod_spec(tm),
            pl.BlockSpec((D, tf), lambda i, f: (0, f)),
            pl.BlockSpec((D, tf), lambda i, f: (0, f)),
            pl.BlockSpec((tf, D), lambda i, f: (f, 0)),
        ],
        out_specs=pl.BlockSpec((tm, D), lambda i, f: (i, 0)),
        out_shape=jax.ShapeDtypeStruct((M, D), F32),
        scratch_shapes=[pltpu.VMEM((tm, D), BF16), pltpu.VMEM((tm, D), F32)],
        compiler_params=_cparams(("arbitrary", "arbitrary"), est),
        name="ffn_dense",
    )(X, gain, mod, wg, wu, wd)


def _router_kernel(x_ref, g_ref, mod_ref, wr_ref, h_ref, r_ref):
    h = _norm_mod(x_ref[...], g_ref[...], mod_ref[3:4, :], mod_ref[4:5, :])
    h_ref[...] = h.astype(BF16)
    h_hi = h.astype(BF16)
    h_lo = (h - h_hi.astype(F32)).astype(BF16)
    wr = wr_ref[...]
    w_hi = wr.astype(BF16)
    w_lo = (wr - w_hi.astype(F32)).astype(BF16)
    logits = _bdot(h_hi, w_hi) + _bdot(h_lo, w_hi) + _bdot(h_hi, w_lo)
    lane = lax.broadcasted_iota(jnp.int32, logits.shape, 1)
    logits = jnp.where(lane < N_EXP, logits, -jnp.inf)
    m1 = jnp.max(logits, -1, keepdims=True)
    i1 = jnp.min(jnp.where(logits == m1, lane, 128), -1, keepdims=True)
    rest = jnp.where(lane == i1, -jnp.inf, logits)
    m2 = jnp.max(rest, -1, keepdims=True)
    i2 = jnp.min(jnp.where(rest == m2, lane, 128), -1, keepdims=True)
    e2 = jnp.exp(m2 - m1)
    w1 = 1.0 / (1.0 + e2)
    w2 = e2 / (1.0 + e2)
    r = jnp.where(lane == 0, i1.astype(F32), 0.0)
    r = jnp.where(lane == 1, i2.astype(F32), r)
    r = jnp.where(lane == 2, w1, r)
    r = jnp.where(lane == 3, w2, r)
    r_ref[...] = r


def _router(X, gain, mod, wr_pad):
    tm = 512
    est = 2 * (tm * D * 4 + D * 128 * 4 + tm * D * 2 + tm * 128 * 4) + 4 * tm * D * 4
    return pl.pallas_call(
        _router_kernel,
        grid=(MX // tm,),
        in_specs=[
            pl.BlockSpec((tm, D), lambda i: (i, 0)),
            pl.BlockSpec((1, D), lambda i: (0, 0)),
            _mod_spec(tm),
            pl.BlockSpec((D, 128), lambda i: (0, 0)),
        ],
        out_specs=[pl.BlockSpec((tm, D), lambda i: (i, 0)), pl.BlockSpec((tm, 128), lambda i: (i, 0))],
        out_shape=[jax.ShapeDtypeStruct((MX, D), BF16), jax.ShapeDtypeStruct((MX, 128), F32)],
        compiler_params=_cparams(("arbitrary",), est),
        name="moe_router",
    )(X, gain, mod, wr_pad)


MOE_TM = 512
MOE_TF = 1792
MOE_ROWS = MX * TOP_K + N_EXP * MOE_TM
MOE_TILES = MOE_ROWS // MOE_TM


def _moe_kernel(te_ref, nt_ref, h_ref, rw_ref, wg_ref, wu_ref, wd_ref, o_ref, acc_ref, *, nf):
    t = pl.program_id(0)
    f = pl.program_id(1)
    valid = t < nt_ref[0]

    @pl.when(jnp.logical_and(valid, f == 0))
    def _():
        acc_ref[...] = jnp.zeros_like(acc_ref)

    @pl.when(valid)
    def _():
        h = h_ref[...]
        a = _bdot(h, wg_ref[...])
        up = _bdot(h, wu_ref[...])
        act = (a * jax.nn.sigmoid(a) * up).astype(BF16)
        acc_ref[...] += _bdot(act, wd_ref[...])

    @pl.when(jnp.logical_and(valid, f == nf - 1))
    def _():
        o_ref[...] = (acc_ref[...] * rw_ref[...]).astype(o_ref.dtype)

    @pl.when(jnp.logical_and(jnp.logical_not(valid), f == nf - 1))
    def _():
        o_ref[...] = jnp.zeros_like(o_ref)


def _moe_experts(tile_expert, n_tiles, h_sorted, row_w, wg, wu, wd):
    tm, tf = MOE_TM, MOE_TF
    nf = E_FF // tf

    def fsel(t, f, nt):
        return jnp.where(t < nt[0], f, nf - 1)

    est = 2 * (tm * D * 2 + tm * 128 * 4 + 3 * D * tf * 2 + tm * D * 2) + tm * D * 4 + 3 * tm * tf * 4
    grid_spec = pltpu.PrefetchScalarGridSpec(
        num_scalar_prefetch=2,
        grid=(MOE_TILES, nf),
        in_specs=[
            pl.BlockSpec((tm, D), lambda t, f, te, nt: (t, 0)),
            pl.BlockSpec((tm, 1), lambda t, f, te, nt: (t, 0)),
            pl.BlockSpec((None, D, tf), lambda t, f, te, nt: (te[t], 0, fsel(t, f, nt))),
            pl.BlockSpec((None, D, tf), lambda t, f, te, nt: (te[t], 0, fsel(t, f, nt))),
            pl.BlockSpec((None, tf, D), lambda t, f, te, nt: (te[t], fsel(t, f, nt), 0)),
        ],
        out_specs=pl.BlockSpec((tm, D), lambda t, f, te, nt: (t, 0)),
        scratch_shapes=[pltpu.VMEM((tm, D), F32)],
    )
    return pl.pallas_call(
        functools.partial(_moe_kernel, nf=nf),
        grid_spec=grid_spec,
        out_shape=jax.ShapeDtypeStruct((MOE_ROWS, D), BF16),
        compiler_params=_cparams(("arbitrary", "arbitrary"), est),
        name="moe_experts",
    )(tile_expert, n_tiles, h_sorted, row_w, wg, wu, wd)


def _final_kernel(x_ref, mod_ref, y0_ref, y1_ref, g_ref, o_ref):
    x = x_ref[...] + mod_ref[5:6, :] * (y0_ref[...].astype(F32) + y1_ref[...].astype(F32))
    o_ref[...] = x * lax.rsqrt(jnp.mean(x * x, -1, keepdims=True) + EPS) * g_ref[...]


def _final(X, mod, y0, y1, gain):
    tm = 512
    est = 2 * (2 * tm * D * 4 + 2 * tm * D * 2) + 3 * tm * D * 4
    return pl.pallas_call(
        _final_kernel,
        grid=(MX // tm,),
        in_specs=[
            pl.BlockSpec((tm, D), lambda i: (i, 0)),
            _mod_spec(tm),
            pl.BlockSpec((tm, D), lambda i: (i, 0)),
            pl.BlockSpec((tm, D), lambda i: (i, 0)),
            pl.BlockSpec((1, D), lambda i: (0, 0)),
        ],
        out_specs=pl.BlockSpec((tm, D), lambda i: (i, 0)),
        out_shape=jax.ShapeDtypeStruct((MX, D), F32),
        compiler_params=_cparams(("arbitrary",), est),
        name="final_norm",
    )(X, mod, y0, y1, gain)


def _moe_routing(route):
    tm = MOE_TM
    e_flat = route[:, 0:TOP_K].astype(jnp.int32).reshape(-1)
    w_flat = route[:, TOP_K:2 * TOP_K].reshape(-1)
    n = e_flat.shape[0]
    order = jnp.argsort(e_flat, stable=True).astype(jnp.int32)
    e_sorted = e_flat[order]
    counts = jnp.sum(e_flat[:, None] == jnp.arange(N_EXP, dtype=jnp.int32)[None, :], 0).astype(jnp.int32)
    padded = ((counts + tm - 1) // tm) * tm
    start = jnp.cumsum(counts) - counts
    pstart = jnp.cumsum(padded) - padded
    dest = pstart[e_sorted] + (jnp.arange(n, dtype=jnp.int32) - start[e_sorted])
    row_token = jnp.zeros((MOE_ROWS,), jnp.int32).at[dest].set(order // TOP_K)
    row_w = jnp.zeros((MOE_ROWS,), F32).at[dest].set(w_flat[order])
    slot_row = jnp.zeros((n,), jnp.int32).at[order].set(dest)
    pend = jnp.cumsum(padded)
    n_tiles = (pend[-1] // tm).astype(jnp.int32)
    tile_start = jnp.arange(MOE_TILES, dtype=jnp.int32) * tm
    tile_expert = jnp.sum(tile_start[:, None] >= pend[None, :], 1).astype(jnp.int32)
    last_expert = tile_expert[jnp.maximum(n_tiles - 1, 0)]
    tile_expert = jnp.where(tile_start < pend[-1], tile_expert, last_expert)
    return row_token, row_w.reshape(MOE_ROWS, 1), slot_row.reshape(MX, TOP_K), tile_expert, n_tiles.reshape(1)


def _dft_table(n):
    k = jnp.arange(n, dtype=jnp.int32)
    ang = ((k[:, None] * k[None, :]) % n).astype(F32) * (2.0 * np.pi / n)
    scale = n ** -0.5
    return jnp.concatenate([jnp.cos(ang) * scale, jnp.sin(ang) * scale], 1).astype(BF16)


def _channel_table():
    k = jnp.arange(F_GDIM, dtype=jnp.int32)
    ang = ((k[:, None] * k[None, :]) % F_GDIM).astype(F32) * (2.0 * np.pi / F_GDIM)
    scale = F_GDIM ** -0.5
    eye = jnp.eye(F_GROUPS, dtype=F32)
    return jnp.concatenate([jnp.kron(eye, jnp.cos(ang) * scale), jnp.kron(eye, -jnp.sin(ang) * scale)], 1).astype(BF16)


def _rope_tables():
    t = jnp.arange(T, dtype=jnp.int32)
    r = (t // GRID_W).astype(F32)[:, None]
    c = (t % GRID_W).astype(F32)[:, None]
    axis_dim = ROPE // 2
    inv = ROPE_BASE ** (-jnp.arange(0, axis_dim, 2, dtype=F32) / axis_dim)[None, :]
    cr, sr, cc, sc = jnp.cos(r * inv), jnp.sin(r * inv), jnp.cos(c * inv), jnp.sin(c * inv)
    cos32 = jnp.concatenate([cr, cr, cc, cc], 1)
    sin32 = jnp.concatenate([-sr, sr, -sc, sc], 1)
    ones = jnp.ones((T, NOPE), F32)
    zeros = jnp.zeros((T, NOPE), F32)
    tail = jnp.zeros((T, HP - NOPE - ROPE), F32)
    cos_x = jnp.concatenate([ones, cos32, tail], 1)
    sin_x = jnp.concatenate([zeros, sin32, tail], 1)
    cos_c = jnp.concatenate([jnp.ones((MC, NOPE + ROPE), F32), jnp.zeros((MC, HP - NOPE - ROPE), F32)], 1)
    cos_t = jnp.concatenate([jnp.tile(cos_x, (B, 1)), cos_c], 0)
    sin_t = jnp.concatenate([jnp.tile(sin_x, (B, 1)), jnp.zeros((MC, HP), F32)], 0)
    return cos_t, sin_t


_ROPE_SWAP = np.concatenate([np.arange(8, 16), np.arange(0, 8), np.arange(24, 32), np.arange(16, 24)])


def _layer_weights(w_in, w_qup, w_kvup, w_branch):
    sizes = (DN_W, DN_W, DN_W, DN_W, 2 * H_DN, 2 * H_DN, Q_RANK, KV_RANK, ROPE, F_W, 3 * D)
    offs = np.concatenate([[0], np.cumsum(sizes)])
    seg = [w_in[:, offs[i]:offs[i + 1]] for i in range(len(sizes))]
    dq, dk, dv, dz, da, db, cq, ckv, kpe, fo, gates = seg
    zc = lambda n: jnp.zeros((D, n), F32)
    w_main = jnp.concatenate([gates, dq, dk, dv, dz, fo, cq, zc(CQ_PAD - Q_RANK), ckv], 1).astype(BF16)
    w_small = jnp.concatenate([da, db, zc(NOPE - 4 * H_DN), kpe, zc(HP - NOPE - ROPE),
                               zc(NOPE), kpe[:, _ROPE_SWAP], zc(HP - NOPE - ROPE)], 1).astype(BF16)

    qh = w_qup.reshape(Q_RANK, H_MLA, NOPE + ROPE)
    zq = lambda n: jnp.zeros((Q_RANK, H_MLA, n), F32)
    wq1 = jnp.concatenate([qh, zq(HP - NOPE - ROPE)], 2).reshape(Q_RANK, H_MLA * HP)
    wq2 = jnp.concatenate([zq(NOPE), qh[:, :, NOPE:][:, :, _ROPE_SWAP], zq(HP - NOPE - ROPE)], 2).reshape(Q_RANK, H_MLA * HP)
    rowpad = jnp.zeros((CQ_PAD - Q_RANK, H_MLA * HP), F32)
    wq1 = jnp.concatenate([wq1, rowpad], 0).astype(BF16)
    wq2 = jnp.concatenate([wq2, rowpad], 0).astype(BF16)
    kvh = w_kvup.reshape(KV_RANK, H_MLA, NOPE + V_DIM)
    zk = jnp.zeros((KV_RANK, H_MLA, HP - NOPE), F32)
    wk = jnp.concatenate([kvh[:, :, :NOPE], zk], 2).reshape(KV_RANK, H_MLA * HP).astype(BF16)
    wv = jnp.concatenate([kvh[:, :, NOPE:], zk], 2).reshape(KV_RANK, H_MLA * HP).astype(BF16)

    wb = w_branch[1].reshape(H_MLA, V_DIM, D)
    wb = jnp.concatenate([wb, jnp.zeros((H_MLA, HP - V_DIM, D), F32)], 1).reshape(H_MLA * HP, D)
    return dict(w_main=w_main, w_small=w_small, wq1=wq1, wq2=wq2, wk=wk, wv=wv,
                wa=w_branch[0].astype(BF16), wb=wb.astype(BF16), wc=w_branch[2].astype(BF16))


def kernel(x, c, ctx, c_ctx, w_mod, b_mod, norm_mix, norm_ffn, w_in, dn_conv, dn_a_log, dn_dt_bias, dn_norm, mla_q_norm, mla_kv_norm, mla_w_qup, mla_w_kvup, w_branch, w_out, ffn_w_gate, ffn_w_up, ffn_w_down, moe_router, moe_w_gate, moe_w_up, moe_w_down, final_norm):
    X = jnp.concatenate([x.reshape(MX, D), ctx.reshape(MC, D)], 0)
    mods = _mod_vectors(c, c_ctx, w_mod, b_mod)
    cos_t, sin_t = _rope_tables()
    cs_x = _dft_table(T)
    cs_c = _dft_table(CTX)
    w_ab = _channel_table()

    out = None
    for l in range(DEPTH):
        last = l == DEPTH - 1
        mod = mods[l]
        lw = _layer_weights(w_in[l], mla_w_qup[l], mla_w_kvup[l], w_branch[l])
        u, usm = _inproj(X, norm_mix[l].reshape(1, D), mod, lw["w_main"], lw["w_small"])
        qkv, gb = _dnpre(u, usm, dn_conv[l], dn_a_log[l], dn_dt_bias[l])
        o0, o1 = _dnscan(qkv, gb)
        gq = jnp.concatenate([mla_q_norm[l], jnp.zeros((CQ_PAD - Q_RANK,), F32)]).reshape(1, CQ_PAD)
        q, k, v = _mlaproj(u, usm, cos_t, sin_t, gq, mla_kv_norm[l].reshape(1, KV_RANK),
                           lw["wq1"], lw["wq2"], lw["wk"], lw["wv"])
        yb = _attention(q, k, v, with_ctx_queries=not last)
        ab = _fourier_channels(u, w_ab)
        yc = _fourier_tokens(cs_x, ab, T, 0, 1024, 2048)
        if not last:
            yc = _fourier_tokens(cs_c, ab, CTX, MX, CTX, CTX, prev=yc)
        rows = MX if last else M
        X = _merge(X, mod, o0, o1, u, yb, yc, dn_norm[l].reshape(1, DK),
                   lw["wa"], lw["wb"], lw["wc"], w_out[l].astype(BF16), rows)
        e = l // 2
        if l % 2 == 0:
            X = _ffn(X, norm_ffn[l].reshape(1, D), mod, ffn_w_gate[e].astype(BF16),
                     ffn_w_up[e].astype(BF16), ffn_w_down[e].astype(BF16))
        else:
            wr = jnp.concatenate([moe_router[e], jnp.zeros((D, 128 - N_EXP), F32)], 1)
            h, route = _router(X, norm_ffn[l].reshape(1, D), mod, wr)
            row_token, row_w, slot_row, tile_expert, n_tiles = _moe_routing(route)
            h_sorted = jnp.take(h, row_token, axis=0)
            y = _moe_experts(tile_expert, n_tiles, h_sorted, row_w, moe_w_gate[e].astype(BF16),
                             moe_w_up[e].astype(BF16), moe_w_down[e].astype(BF16))
            y0 = jnp.take(y, slot_row[:, 0], axis=0)
            y1 = jnp.take(y, slot_row[:, 1], axis=0)
            out = _final(X, mod, y0, y1, final_norm.reshape(1, D))
    return out.reshape(B, T, D)
```

```python
import functools

import numpy as np
import jax
import jax.numpy as jnp
from jax import lax
from jax.experimental import pallas as pl
from jax.experimental.pallas import tpu as pltpu

F32 = jnp.float32
BF16 = jnp.bfloat16

D = 1024
B = 4
T = 4096
CTX = 256
DEPTH = 2
GRID_W = 64
H_DN = 4
DK = 128
DN_W = H_DN * DK
CONV_K = 5
CHUNK = 64
H_MLA = 8
Q_RANK = 384
KV_RANK = 256
NOPE = 64
ROPE = 32
V_DIM = 64
QK_SCALE = (NOPE + ROPE) ** -0.5
ROPE_BASE = 10000.0
F_GROUPS = 4
F_GDIM = 128
F_W = F_GROUPS * F_GDIM
D_FF = 2816
N_EXP = 8
TOP_K = 2
E_FF = 3584
EPS = 1e-6

MX = B * T
MC = B * CTX
M = MX + MC

HP = 128
C_GATES = 0
C_QKV = 3 * D
C_Z = C_QKV + 3 * DN_W
C_F = C_Z + DN_W
C_CQ = C_F + F_W
CQ_PAD = 512
C_CKV = C_CQ + CQ_PAD
N_MAIN = C_CKV + KV_RANK
N_SMALL = 256

V7X_VMEM_BYTES = 64 * 1024 * 1024
V7X_VMEM_USABLE = 56 * 1024 * 1024


def _cparams(sem, vmem_est_bytes):
    limit = int(min(V7X_VMEM_USABLE, max(32 * 1024 * 1024, 2 * vmem_est_bytes)))
    return pltpu.CompilerParams(dimension_semantics=sem, vmem_limit_bytes=limit)


def _norm_mod(x, gain, shift, scale):
    y = x * lax.rsqrt(jnp.mean(x * x, -1, keepdims=True) + EPS) * gain
    return y * (1.0 + scale) + shift


def _bdot(a, b):
    return jnp.dot(a, b, preferred_element_type=F32)


def _split3(v):
    hi = v.astype(BF16)
    r1 = v - hi.astype(F32)
    mid = r1.astype(BF16)
    lo = (r1 - mid.astype(F32)).astype(BF16)
    return hi, mid, lo


def _mod_kernel(a_ref, w_ref, b_ref, o_ref):
    a = a_ref[...]
    a = a * jax.nn.sigmoid(a)
    w = w_ref[...]
    a_hi = a.astype(BF16)
    a_lo = (a - a_hi.astype(F32)).astype(BF16)
    w_hi = w.astype(BF16)
    w_lo = (w - w_hi.astype(F32)).astype(BF16)
    o_ref[...] = _bdot(a_hi, w_hi) + _bdot(a_lo, w_hi) + _bdot(a_hi, w_lo) + b_ref[...]


def _mod_vectors(c, c_ctx, w_mod, b_mod):
    tn = 1536
    a = jnp.concatenate([c, c_ctx[None, :], jnp.zeros((3, D), F32)], 0)
    out = pl.pallas_call(
        _mod_kernel,
        grid=(DEPTH, 6 * D // tn),
        in_specs=[
            pl.BlockSpec((8, D), lambda l, j: (0, 0)),
            pl.BlockSpec((None, D, tn), lambda l, j: (l, 0, j)),
            pl.BlockSpec((None, 1, tn), lambda l, j: (l, 0, j)),
        ],
        out_specs=pl.BlockSpec((None, 8, tn), lambda l, j: (l, 0, j)),
        out_shape=jax.ShapeDtypeStruct((DEPTH, 8, 6 * D), F32),
        compiler_params=_cparams(("arbitrary", "arbitrary"), 2 * D * tn * 4 * 2),
        name="mod_vectors",
    )(a, w_mod, b_mod.reshape(DEPTH, 1, 6 * D))
    return out.reshape(DEPTH, 8, 6, D)[:, : B + 1]


def _mod_spec(tm):
    tpb = T // tm
    return pl.BlockSpec((None, 6, D), lambda i, *_: (jnp.minimum(i // tpb, B), 0, 0))


def _inproj_kernel(x_ref, g_ref, mod_ref, w_ref, ws_ref, o_ref, os_ref, h_ref):
    @pl.when(pl.program_id(1) == 0)
    def _():
        hb = _norm_mod(x_ref[...], g_ref[...], mod_ref[0:1, :], mod_ref[1:2, :]).astype(BF16)
        h_ref[...] = hb
        os_ref[...] = _bdot(hb, ws_ref[...])

    o_ref[...] = _bdot(h_ref[...], w_ref[...]).astype(o_ref.dtype)


def _inproj(X, gain, mod, w_main, w_small):
    tm, tn = 512, 1280
    est = 2 * (tm * D * 4 + D * tn * 2 + D * N_SMALL * 2 + tm * tn * 2 + tm * N_SMALL * 4) + tm * D * 2
    return pl.pallas_call(
        _inproj_kernel,
        grid=(M // tm, N_MAIN // tn),
        in_specs=[
            pl.BlockSpec((tm, D), lambda i, j: (i, 0)),
            pl.BlockSpec((1, D), lambda i, j: (0, 0)),
            _mod_spec(tm),
            pl.BlockSpec((D, tn), lambda i, j: (0, j)),
            pl.BlockSpec((D, N_SMALL), lambda i, j: (0, 0)),
        ],
        out_specs=[
            pl.BlockSpec((tm, tn), lambda i, j: (i, j)),
            pl.BlockSpec((tm, N_SMALL), lambda i, j: (i, 0)),
        ],
        out_shape=[jax.ShapeDtypeStruct((M, N_MAIN), BF16), jax.ShapeDtypeStruct((M, N_SMALL), F32)],
        scratch_shapes=[pltpu.VMEM((tm, D), BF16)],
        compiler_params=_cparams(("arbitrary", "arbitrary"), est),
        name="in_proj",
    )(X, gain, mod, w_main, w_small)


DN_TT = 256
DN_HALO = 16


def _dnpre_kernel(cur_ref, prev_ref, next_ref, sm_ref, cw_ref, al_ref, dtb_ref, qkv_ref, gb_ref):
    i = pl.program_id(0)
    tiles_per_seq = T // DN_TT
    is_ctx = i >= MX // DN_TT
    first = jnp.logical_or(is_ctx, i % tiles_per_seq == 0)
    last = jnp.logical_or(is_ctx, i % tiles_per_seq == tiles_per_seq - 1)
    prev = jnp.where(first, 0.0, prev_ref[...].astype(F32))
    nxt = jnp.where(last, 0.0, next_ref[...].astype(F32))
    xc = jnp.concatenate([prev, cur_ref[...].astype(F32), nxt], axis=0)
    n = xc.shape[0]
    half = CONV_K // 2
    acc = xc * cw_ref[half:half + 1, :]
    for j in range(CONV_K):
        if j != half:
            acc = acc + pltpu.roll(xc, (half - j) % n, 0) * cw_ref[j:j + 1, :]
    y = acc[DN_HALO:DN_HALO + DN_TT]
    y = y * jax.nn.sigmoid(y)
    for p in range(3 * H_DN):
        blk = y[:, p * DK:(p + 1) * DK]
        if p < 2 * H_DN:
            blk = blk * lax.rsqrt(jnp.sum(blk * blk, -1, keepdims=True) + EPS)
        qkv_ref[:, p * DK:(p + 1) * DK] = blk.astype(BF16)

    sm = sm_ref[...]
    lane = lax.broadcasted_iota(jnp.int32, sm.shape, 1)
    g = -jnp.exp(al_ref[...]) * jax.nn.softplus(sm + dtb_ref[...])
    gb_ref[...] = jnp.where(lane < 2 * H_DN, g, jax.nn.sigmoid(sm))


def _dnpre(u, usm, conv_w, a_log, dt_bias):
    tt = DN_TT
    nb16 = tt // DN_HALO
    qkv_cb = C_QKV // (3 * DN_W)
    pad = jnp.zeros((1, 128 - 2 * H_DN), F32)
    al = jnp.concatenate([a_log.reshape(1, 2 * H_DN), pad], 1)
    dtb = jnp.concatenate([dt_bias.reshape(1, 2 * H_DN), pad], 1)
    est = 2 * (tt * 1536 * 2 * 2 + tt * 128 * 4 * 2) + 6 * (tt + 32) * 1536 * 4
    return pl.pallas_call(
        _dnpre_kernel,
        grid=(M // tt,),
        in_specs=[
            pl.BlockSpec((tt, 3 * DN_W), lambda i: (i, qkv_cb)),
            pl.BlockSpec((DN_HALO, 3 * DN_W), lambda i: (jnp.maximum(i * nb16 - 1, 0), qkv_cb)),
            pl.BlockSpec((DN_HALO, 3 * DN_W), lambda i: (jnp.minimum((i + 1) * nb16, M // DN_HALO - 1), qkv_cb)),
            pl.BlockSpec((tt, 128), lambda i: (i, 0)),
            pl.BlockSpec((CONV_K, 3 * DN_W), lambda i: (0, 0)),
            pl.BlockSpec((1, 128), lambda i: (0, 0)),
            pl.BlockSpec((1, 128), lambda i: (0, 0)),
        ],
        out_specs=[
            pl.BlockSpec((tt, 3 * DN_W), lambda i: (i, 0)),
            pl.BlockSpec((tt, 128), lambda i: (i, 0)),
        ],
        out_shape=[jax.ShapeDtypeStruct((M, 3 * DN_W), BF16), jax.ShapeDtypeStruct((M, 128), F32)],
        compiler_params=_cparams(("arbitrary",), est),
        name="dn_pre",
    )(u, u, u, usm, conv_w, al, dtb)


N_XCH = T // CHUNK
N_CCH = CTX // CHUNK
N_STEPS = N_XCH + N_CCH


def _dnscan_kernel(qkv0_ref, qkv1_ref, gb0_ref, gb1_ref, o0_ref, o1_ref, s_ref):
    @pl.when(pl.program_id(1) == 0)
    def _():
        s_ref[...] = jnp.zeros_like(s_ref)

    C = CHUNK
    W2 = 2 * DK
    ri = lax.broadcasted_iota(jnp.int32, (C, 2 * C), 0)
    li = lax.broadcasted_iota(jnp.int32, (C, 2 * C), 1)
    lm = li % C
    first = li < C
    firstw = lax.broadcasted_iota(jnp.int32, (C, W2), 1) < DK
    eye2 = (ri == lm).astype(F32)
    ii = lax.broadcasted_iota(jnp.int32, (C, C), 0)
    jj = lax.broadcasted_iota(jnp.int32, (C, C), 1)
    sr = lax.broadcasted_iota(jnp.int32, (W2, W2), 0) < DK
    sc = lax.broadcasted_iota(jnp.int32, (W2, W2), 1) < DK
    sdiag = sr == sc
    qscale = DK ** -0.5
    nt = (((1,), (1,)), ((), ()))
    tn = (((0,), (0,)), ((), ()))

    def bd(m2):
        return jnp.concatenate([jnp.where(first, m2, 0.0), jnp.where(first, 0.0, m2)], 0).astype(BF16)

    def bdw(m2):
        return jnp.concatenate([jnp.where(firstw, m2, 0.0), jnp.where(firstw, 0.0, m2)], 0).astype(BF16)

    ch = []
    for d in range(2):
        qkv_ref = (qkv0_ref, qkv1_ref)[d]
        gbv = (gb0_ref, gb1_ref)[d][...]
        incl = (ri >= lm) if d == 0 else (ri <= lm)
        strict = (ri > lm) if d == 0 else (ri < lm)
        tri = ((ii >= jj) if d == 0 else (ii <= jj)).astype(BF16)
        g_hi, g_mid, g_lo = _split3(gbv)
        gc = _bdot(tri, g_hi) + _bdot(tri, g_mid) + _bdot(tri, g_lo)
        gct = jnp.concatenate([gc, gc], 0).T
        last_row = C - 1 if d == 0 else 0
        for p in range(H_DN // 2):
            la = d * H_DN + 2 * p
            lb = la + 1
            col2 = jnp.where(first, gc[:, la:la + 1], gc[:, lb:lb + 1])
            row2 = jnp.where(first[0:1], gct[la:la + 1, :], gct[lb:lb + 1, :])
            colw = jnp.where(firstw, gc[:, la:la + 1], gc[:, lb:lb + 1])
            betaw = jnp.where(firstw, gbv[:, 2 * H_DN + la:2 * H_DN + la + 1], gbv[:, 2 * H_DN + lb:2 * H_DN + lb + 1])
            gl_a = gc[last_row:last_row + 1, la:la + 1]
            gl_b = gc[last_row:last_row + 1, lb:lb + 1]
            k2 = qkv_ref[:, DN_W + p * W2:DN_W + (p + 1) * W2].astype(F32)
            kb = k2 * betaw
            qs = qkv_ref[:, p * W2:(p + 1) * W2].astype(F32) * qscale
            ecolw = jnp.exp(colw)
            vbeta = qkv_ref[:, 2 * DN_W + p * W2:2 * DN_W + (p + 1) * W2].astype(F32) * betaw
            kbe = kb * ecolw
            ch.append(dict(
                d=d, p=p, strict=strict,
                dec=jnp.exp(jnp.where(incl, col2 - row2, -jnp.inf)),
                lhs=jnp.concatenate([kb.astype(BF16), qs.astype(BF16)], 0),
                kbd=bdw(k2),
                qg=(qs * ecolw).astype(BF16),
                kd=(k2 * jnp.exp(jnp.where(firstw, gl_a, gl_b) - colw)).astype(BF16),
                rhs=jnp.concatenate([
                    jnp.concatenate([vbeta[:, 0:DK], kbe[:, 0:DK]], 1),
                    jnp.concatenate([vbeta[:, DK:W2], kbe[:, DK:W2]], 1)], 0).astype(BF16),
                gls=jnp.where(sr, jnp.exp(gl_a), jnp.exp(gl_b)),
            ))

    for c in ch:
        c["kq"] = lax.dot_general(c["lhs"], c["kbd"], nt, preferred_element_type=F32)
    for c in ch:
        kq = c["kq"]
        nmat = -jnp.where(c["strict"], kq[0:C] * c["dec"], 0.0)
        c["attn"] = (kq[C:2 * C] * c["dec"]).astype(BF16)
        c["x"] = eye2 + nmat
        c["nm"] = nmat
    for c in ch:
        c["pw"] = _bdot(c["nm"].astype(BF16), bd(c["nm"]))
    for it in range(5):
        for c in ch:
            c["pbd"] = bd(c["pw"])
        for c in ch:
            c["x"] = c["x"] + _bdot(c["x"].astype(BF16), c["pbd"])
        if it < 4:
            for c in ch:
                c["pw"] = _bdot(c["pw"].astype(BF16), c["pbd"])
    for c in ch:
        x = c["x"]
        xl = jnp.concatenate([jnp.where(first, x, 0.0), jnp.where(first, 0.0, x)], 0).astype(BF16)
        uw = _bdot(xl, c["rhs"])
        c["u"] = jnp.concatenate([uw[0:C, 0:DK], uw[C:2 * C, 0:DK]], 1)
        w = jnp.concatenate([uw[0:C, DK:W2], uw[C:2 * C, DK:W2]], 1)
        c["wq"] = jnp.concatenate([w.astype(BF16), c["qg"]], 0)
    for c in ch:
        c["s"] = s_ref[c["d"] * 2 + c["p"]]
        c["wqs"] = _bdot(c["wq"], c["s"].astype(BF16))
    for c in ch:
        c["vn"] = c["u"] - c["wqs"][0:C]
    for c in ch:
        o = c["wqs"][C:2 * C] + _bdot(c["attn"], bdw(c["vn"]))
        o_ref = (o0_ref, o1_ref)[c["d"]]
        o_ref[:, c["p"] * W2:(c["p"] + 1) * W2] = o.astype(o_ref.dtype)
    for c in ch:
        kv = lax.dot_general(c["kd"], c["vn"].astype(BF16), tn, preferred_element_type=F32)
        s_ref[c["d"] * 2 + c["p"]] = c["s"] * c["gls"] + jnp.where(sdiag, kv, 0.0)


def _dnscan(qkv, gb):
    xblk = MX // CHUNK

    def idx0(b, s):
        return jnp.where(s < N_CCH, xblk + b * N_CCH + s, b * N_XCH + s - N_CCH)

    def idx1(b, s):
        return jnp.where(s < N_CCH, xblk + b * N_CCH + (N_CCH - 1 - s), b * N_XCH + (N_STEPS - 1 - s))

    est = 2 * 2 * (CHUNK * 1536 * 2 + CHUNK * 128 * 4 + CHUNK * 512 * 4) + 8 * DK * DK * 4 + (4 << 20)
    return pl.pallas_call(
        _dnscan_kernel,
        grid=(B, N_STEPS),
        in_specs=[
            pl.BlockSpec((CHUNK, 3 * DN_W), lambda b, s: (idx0(b, s), 0)),
            pl.BlockSpec((CHUNK, 3 * DN_W), lambda b, s: (idx1(b, s), 0)),
            pl.BlockSpec((CHUNK, 128), lambda b, s: (idx0(b, s), 0)),
            pl.BlockSpec((CHUNK, 128), lambda b, s: (idx1(b, s), 0)),
        ],
        out_specs=[
            pl.BlockSpec((CHUNK, DN_W), lambda b, s: (idx0(b, s), 0)),
            pl.BlockSpec((CHUNK, DN_W), lambda b, s: (idx1(b, s), 0)),
        ],
        out_shape=[jax.ShapeDtypeStruct((M, DN_W), F32), jax.ShapeDtypeStruct((M, DN_W), F32)],
        scratch_shapes=[pltpu.VMEM((H_DN, 2 * DK, 2 * DK), F32)],
        compiler_params=_cparams(("arbitrary", "arbitrary"), est),
        name="dn_scan",
    )(qkv, qkv, gb, gb)


def _mlaproj_kernel(cq_ref, ckv_ref, sm_ref, cos_ref, sin_ref, gq_ref, gkv_ref,
                    wq1_ref, wq2_ref, wk_ref, wv_ref, q_ref, k_ref, v_ref):
    cq = cq_ref[...].astype(F32)
    hq = (cq * lax.rsqrt(jnp.sum(cq * cq, -1, keepdims=True) * (1.0 / Q_RANK) + EPS) * gq_ref[...]).astype(BF16)
    ckv = ckv_ref[...].astype(F32)
    hkv = (ckv * lax.rsqrt(jnp.mean(ckv * ckv, -1, keepdims=True) + EPS) * gkv_ref[...]).astype(BF16)
    q1 = _bdot(hq, wq1_ref[...])
    q2 = _bdot(hq, wq2_ref[...])
    kn = _bdot(hkv, wk_ref[...])
    vv = _bdot(hkv, wv_ref[...])
    cosq = cos_ref[...]
    sin = sin_ref[...]
    lane = lax.broadcasted_iota(jnp.int32, cosq.shape, 1)
    cosk = jnp.where(lane >= NOPE, cosq, 0.0)
    kpe = sm_ref[:, 0:HP] * cosk + sm_ref[:, HP:2 * HP] * sin
    for h in range(H_MLA):
        sl = slice(h * HP, (h + 1) * HP)
        q_ref[:, sl] = ((q1[:, sl] * cosq + q2[:, sl] * sin) * QK_SCALE).astype(BF16)
        k_ref[:, sl] = (kn[:, sl] + kpe).astype(BF16)
        v_ref[:, sl] = jnp.where(lane == V_DIM, 1.0, vv[:, sl]).astype(BF16)


def _mlaproj(u, usm, cos_t, sin_t, gq, gkv, wq1, wq2, wk, wv):
    tm = 256
    wide = H_MLA * HP
    est = 2 * (tm * (CQ_PAD + KV_RANK) * 2 + tm * 256 * 4 + 2 * tm * 128 * 4 + 2 * CQ_PAD * wide * 2
               + 2 * KV_RANK * wide * 2 + 3 * tm * wide * 2) + 6 * tm * wide * 4
    full = lambda shape: pl.BlockSpec(shape, lambda i: (0, 0))
    return pl.pallas_call(
        _mlaproj_kernel,
        grid=(M // tm,),
        in_specs=[
            pl.BlockSpec((tm, CQ_PAD), lambda i: (i, C_CQ // CQ_PAD)),
            pl.BlockSpec((tm, KV_RANK), lambda i: (i, C_CKV // KV_RANK)),
            pl.BlockSpec((tm, N_SMALL), lambda i: (i, 0)),
            pl.BlockSpec((tm, HP), lambda i: (i, 0)),
            pl.BlockSpec((tm, HP), lambda i: (i, 0)),
            full((1, CQ_PAD)), full((1, KV_RANK)),
            full((CQ_PAD, wide)), full((CQ_PAD, wide)), full((KV_RANK, wide)), full((KV_RANK, wide)),
        ],
        out_specs=[pl.BlockSpec((tm, wide), lambda i: (i, 0))] * 3,
        out_shape=[jax.ShapeDtypeStruct((M, wide), BF16)] * 3,
        compiler_params=_cparams(("arbitrary",), est),
        name="mla_proj",
    )(u, u, usm, cos_t, sin_t, gq, gkv, wq1, wq2, wk, wv)


ATT_TQ = 256


def _attn_kernel(q_ref, kx_ref, vx_ref, kc_ref, vc_ref, o_ref):
    qi = pl.program_id(2)
    q = q_ref[...]
    nt = (((1,), (1,)), ((), ()))
    sc = lax.dot_general(q, kc_ref[...], nt, preferred_element_type=F32)
    mc = jnp.max(sc, -1, keepdims=True)

    @pl.when(qi < T // ATT_TQ)
    def _():
        sx = lax.dot_general(q, kx_ref[...], nt, preferred_element_type=F32)
        m = jnp.maximum(mc, jnp.max(sx, -1, keepdims=True))
        acc = _bdot(jnp.exp(sx - m).astype(BF16), vx_ref[...])
        acc = acc + _bdot(jnp.exp(sc - m).astype(BF16), vc_ref[...])
        o_ref[...] = (acc / acc[:, V_DIM:V_DIM + 1]).astype(o_ref.dtype)

    @pl.when(qi >= T // ATT_TQ)
    def _():
        acc = _bdot(jnp.exp(sc - mc).astype(BF16), vc_ref[...])
        o_ref[...] = (acc / acc[:, V_DIM:V_DIM + 1]).astype(o_ref.dtype)


def _attention(q, k, v, with_ctx_queries):
    tq = ATT_TQ
    nqx = T // tq
    nq = nqx + (1 if with_ctx_queries else 0)
    cblk = MX // CTX
    rows = M if with_ctx_queries else MX

    def qidx(b, h, i):
        return (jnp.where(i < nqx, b * nqx + i, MX // tq + b), h)

    est = 2 * (2 * T * HP * 2 + 2 * CTX * HP * 2 + 2 * tq * HP * 2) + 3 * tq * (T + CTX) * 4
    return pl.pallas_call(
        _attn_kernel,
        grid=(B, H_MLA, nq),
        in_specs=[
            pl.BlockSpec((tq, HP), qidx),
            pl.BlockSpec((T, HP), lambda b, h, i: (b, h)),
            pl.BlockSpec((T, HP), lambda b, h, i: (b, h)),
            pl.BlockSpec((CTX, HP), lambda b, h, i: (cblk + b, h)),
            pl.BlockSpec((CTX, HP), lambda b, h, i: (cblk + b, h)),
        ],
        out_specs=pl.BlockSpec((tq, HP), qidx),
        out_shape=jax.ShapeDtypeStruct((rows, H_MLA * HP), BF16),
        compiler_params=_cparams(("arbitrary", "arbitrary", "arbitrary"), est),
        name="mla_attention",
    )(q, k, v, k, v)


def _mm1_kernel(a_ref, b_ref, o_ref):
    o_ref[...] = _bdot(a_ref[...], b_ref[...]).astype(o_ref.dtype)


def _fourier_channels(u, w_ab):
    tm = 512
    est = 2 * (tm * F_W * 2 + F_W * 2 * F_W * 2 + tm * 2 * F_W * 2)
    return pl.pallas_call(
        _mm1_kernel,
        grid=(M // tm,),
        in_specs=[pl.BlockSpec((tm, F_W), lambda i: (i, C_F // F_W)),
                  pl.BlockSpec((F_W, 2 * F_W), lambda i: (0, 0))],
        out_specs=pl.BlockSpec((tm, 2 * F_W), lambda i: (i, 0)),
        out_shape=jax.ShapeDtypeStruct((M, 2 * F_W), BF16),
        compiler_params=_cparams(("arbitrary",), est),
        name="fourier_channels",
    )(u, w_ab)


def _dft_kernel(cs_ref, ab_ref, *rest, nk):
    o_ref, acc_ref = rest[-2], rest[-1]
    kk = pl.program_id(2)

    @pl.when(kk == 0)
    def _():
        acc_ref[...] = jnp.zeros_like(acc_ref)

    acc_ref[...] += _bdot(cs_ref[...], ab_ref[...])

    @pl.when(kk == nk - 1)
    def _():
        o_ref[...] = acc_ref[...].astype(o_ref.dtype)


def _fourier_tokens(cs, ab, seq, row0, tm, tk, prev=None):
    nk = 2 * seq // tk
    kpb = seq // tk
    in_specs = [
        pl.BlockSpec((tm, tk), lambda b, i, kk: (i, kk)),
        pl.BlockSpec((tk, F_W), lambda b, i, kk: (row0 // tk + b * kpb + kk % kpb, kk // kpb)),
    ]
    args = [cs, ab]
    aliases = {}
    if prev is not None:
        in_specs.append(pl.BlockSpec(memory_space=pl.ANY))
        args.append(prev)
        aliases = {2: 0}
    est = 2 * (tm * tk * 2 + tk * F_W * 2 + tm * F_W * 2) + tm * F_W * 4
    return pl.pallas_call(
        functools.partial(_dft_kernel, nk=nk),
        grid=(B, seq // tm, nk),
        in_specs=in_specs,
        out_specs=pl.BlockSpec((tm, F_W), lambda b, i, kk: (row0 // tm + b * (seq // tm) + i, 0)),
        out_shape=jax.ShapeDtypeStruct((M, F_W), BF16),
        scratch_shapes=[pltpu.VMEM((tm, F_W), F32)],
        input_output_aliases=aliases,
        compiler_params=_cparams(("arbitrary", "arbitrary", "arbitrary"), est),
        name="fourier_tokens",
    )(*args)


def _merge_kernel(x_ref, mod_ref, o0_ref, o1_ref, z_ref, yb_ref, yc_ref, gt_ref, gdn_ref,
                  wa_ref, wb_ref, wc_ref, wo_ref, out_ref):
    o = o0_ref[...] + o1_ref[...]
    z = z_ref[...].astype(F32)
    gdn = gdn_ref[...]
    ya = []
    for h in range(H_DN):
        sl = slice(h * DK, (h + 1) * DK)
        oh = o[:, sl]
        zh = z[:, sl]
        yh = oh * lax.rsqrt(jnp.mean(oh * oh, -1, keepdims=True) + EPS) * gdn
        ya.append((yh * (zh * jax.nn.sigmoid(zh))).astype(BF16))
    ya = jnp.concatenate(ya, -1)
    s = jax.nn.sigmoid(gt_ref[:, 0:D].astype(F32)) * _bdot(ya, wa_ref[...])
    s = s + jax.nn.sigmoid(gt_ref[:, D:2 * D].astype(F32)) * _bdot(yb_ref[...], wb_ref[...])
    s = s + jax.nn.sigmoid(gt_ref[:, 2 * D:3 * D].astype(F32)) * _bdot(yc_ref[...], wc_ref[...])
    mix = _bdot(s.astype(BF16), wo_ref[...])
    out_ref[...] = x_ref[...] + mod_ref[2:3, :] * mix


def _merge(X, mod, o0, o1, u, yb, yc, gdn, wa, wb, wc, wo, rows):
    tm = 256
    full = lambda shape: pl.BlockSpec(shape, lambda i: (0, 0))
    est = 2 * (2 * tm * D * 4 + 2 * tm * DN_W * 4 + tm * DN_W * 2 * 2 + tm * D * 2 + tm * 3 * D * 2
               + (2 * DN_W + 2 * D) * D * 2) + 6 * tm * D * 4
    return pl.pallas_call(
        _merge_kernel,
        grid=(rows // tm,),
        in_specs=[
            pl.BlockSpec((tm, D), lambda i: (i, 0)),
            _mod_spec(tm),
            pl.BlockSpec((tm, DN_W), lambda i: (i, 0)),
            pl.BlockSpec((tm, DN_W), lambda i: (i, 0)),
            pl.BlockSpec((tm, DN_W), lambda i: (i, C_Z // DN_W)),
            pl.BlockSpec((tm, H_MLA * HP), lambda i: (i, 0)),
            pl.BlockSpec((tm, F_W), lambda i: (i, 0)),
            pl.BlockSpec((tm, 3 * D), lambda i: (i, 0)),
            full((1, DK)),
            full((DN_W, D)), full((H_MLA * HP, D)), full((F_W, D)), full((D, D)),
        ],
        out_specs=pl.BlockSpec((tm, D), lambda i: (i, 0)),
        out_shape=jax.ShapeDtypeStruct((rows, D), F32),
        compiler_params=_cparams(("arbitrary",), est),
        name="merge",
    )(X, mod, o0, o1, u, yb, yc, u, gdn, wa, wb, wc, wo)


def _ffn_kernel(x_ref, g_ref, mod_ref, wg_ref, wu_ref, wd_ref, o_ref, h_ref, acc_ref, *, nf):
    f = pl.program_id(1)

    @pl.when(f == 0)
    def _():
        h_ref[...] = _norm_mod(x_ref[...], g_ref[...], mod_ref[3:4, :], mod_ref[4:5, :]).astype(BF16)
        acc_ref[...] = jnp.zeros_like(acc_ref)

    h = h_ref[...]
    a = _bdot(h, wg_ref[...])
    up = _bdot(h, wu_ref[...])
    act = (a * jax.nn.sigmoid(a) * up).astype(BF16)
    acc_ref[...] += _bdot(act, wd_ref[...])

    @pl.when(f == nf - 1)
    def _():
        o_ref[...] = x_ref[...] + mod_ref[5:6, :] * acc_ref[...]


def _ffn(X, gain, mod, wg, wu, wd):
    tm, tf = 512, 1408
    nf = D_FF // tf
    est = 2 * (2 * tm * D * 4 + 3 * D * tf * 2) + tm * D * 6 + 3 * tm * tf * 4
    return pl.pallas_call(
        functools.partial(_ffn_kernel, nf=nf),
        grid=(M // tm, nf),
        in_specs=[
            pl.BlockSpec((tm, D), lambda i, f: (i, 0)),
            pl.BlockSpec((1, D), lambda i, f: (0, 0)),
            _mod_spec(tm),
            pl.BlockSpec((D, tf), lambda i, f: (0, f)),
            pl.BlockSpec((D, tf), lambda i, f: (0, f)),
            pl.BlockSpec((tf, D), lambda i, f: (f, 0)),
        ],
        out_specs=pl.BlockSpec((tm, D), lambda i, f: (i, 0)),
        out_shape=jax.ShapeDtypeStruct((M, D), F32),
        scratch_shapes=[pltpu.VMEM((tm, D), BF16), pltpu.VMEM((tm, D), F32)],
        compiler_params=_cparams(("arbitrary", "arbitrary"), est),
        name="ffn_dense",
    )(X, gain, mod, wg, wu, wd)


def _router_kernel(x_ref, g_ref, mod_ref, wr_ref, h_ref, r_ref):
    h = _norm_mod(x_ref[...], g_ref[...], mod_ref[3:4, :], mod_ref[4:5, :])
    h_ref[...] = h.astype(BF16)
    h_hi = h.astype(BF16)
    h_lo = (h - h_hi.astype(F32)).astype(BF16)
    wr = wr_ref[...]
    w_hi = wr.astype(BF16)
    w_lo = (wr - w_hi.astype(F32)).astype(BF16)
    logits = _bdot(h_hi, w_hi) + _bdot(h_lo, w_hi) + _bdot(h_hi, w_lo)
    lane = lax.broadcasted_iota(jnp.int32, logits.shape, 1)
    logits = jnp.where(lane < N_EXP, logits, -jnp.inf)
    m1 = jnp.max(logits, -1, keepdims=True)
    i1 = jnp.min(jnp.where(logits == m1, lane, 128), -1, keepdims=True)
    rest = jnp.where(lane == i1, -jnp.inf, logits)
    m2 = jnp.max(rest, -1, keepdims=True)
    i2 = jnp.min(jnp.where(rest == m2, lane, 128), -1, keepdims=True)
    e2 = jnp.exp(m2 - m1)
    w1 = 1.0 / (1.0 + e2)
    w2 = e2 / (1.0 + e2)
    r = jnp.where(lane == 0, i1.astype(F32), 0.0)
    r = jnp.where(lane == 1, i2.astype(F32), r)
    r = jnp.where(lane == 2, w1, r)
    r = jnp.where(lane == 3, w2, r)
    r_ref[...] = r


def _router(X, gain, mod, wr_pad):
    tm = 512
    est = 2 * (tm * D * 4 + D * 128 * 4 + tm * D * 2 + tm * 128 * 4) + 4 * tm * D * 4
    return pl.pallas_call(
        _router_kernel,
        grid=(MX // tm,),
        in_specs=[
            pl.BlockSpec((tm, D), lambda i: (i, 0)),
            pl.BlockSpec((1, D), lambda i: (0, 0)),
            _mod_spec(tm),
            pl.BlockSpec((D, 128), lambda i: (0, 0)),
        ],
        out_specs=[pl.BlockSpec((tm, D), lambda i: (i, 0)), pl.BlockSpec((tm, 128), lambda i: (i, 0))],
        out_shape=[jax.ShapeDtypeStruct((MX, D), BF16), jax.ShapeDtypeStruct((MX, 128), F32)],
        compiler_params=_cparams(("arbitrary",), est),
        name="moe_router",
    )(X, gain, mod, wr_pad)


MOE_TM = 512
MOE_TF = 1792
MOE_ROWS = MX * TOP_K + N_EXP * MOE_TM
MOE_TILES = MOE_ROWS // MOE_TM


def _moe_kernel(te_ref, nt_ref, h_ref, rw_ref, wg_ref, wu_ref, wd_ref, o_ref, acc_ref, *, nf):
    t = pl.program_id(0)
    f = pl.program_id(1)
    valid = t < nt_ref[0]

    @pl.when(jnp.logical_and(valid, f == 0))
    def _():
        acc_ref[...] = jnp.zeros_like(acc_ref)

    @pl.when(valid)
    def _():
        h = h_ref[...]
        a = _bdot(h, wg_ref[...])
        up = _bdot(h, wu_ref[...])
        act = (a * jax.nn.sigmoid(a) * up).astype(BF16)
        acc_ref[...] += _bdot(act, wd_ref[...])

    @pl.when(jnp.logical_and(valid, f == nf - 1))
    def _():
        o_ref[...] = (acc_ref[...] * rw_ref[...]).astype(o_ref.dtype)

    @pl.when(jnp.logical_and(jnp.logical_not(valid), f == nf - 1))
    def _():
        o_ref[...] = jnp.zeros_like(o_ref)


def _moe_experts(tile_expert, n_tiles, h_sorted, row_w, wg, wu, wd):
    tm, tf = MOE_TM, MOE_TF
    nf = E_FF // tf

    def fsel(t, f, nt):
        return jnp.where(t < nt[0], f, nf - 1)

    est = 2 * (tm * D * 2 + tm * 128 * 4 + 3 * D * tf * 2 + tm * D * 2) + tm * D * 4 + 3 * tm * tf * 4
    grid_spec = pltpu.PrefetchScalarGridSpec(
        num_scalar_prefetch=2,
        grid=(MOE_TILES, nf),
        in_specs=[
            pl.BlockSpec((tm, D), lambda t, f, te, nt: (t, 0)),
            pl.BlockSpec((tm, 1), lambda t, f, te, nt: (t, 0)),
            pl.BlockSpec((None, D, tf), lambda t, f, te, nt: (te[t], 0, fsel(t, f, nt))),
            pl.BlockSpec((None, D, tf), lambda t, f, te, nt: (te[t], 0, fsel(t, f, nt))),
            pl.BlockSpec((None, tf, D), lambda t, f, te, nt: (te[t], fsel(t, f, nt), 0)),
        ],
        out_specs=pl.BlockSpec((tm, D), lambda t, f, te, nt: (t, 0)),
        scratch_shapes=[pltpu.VMEM((tm, D), F32)],
    )
    return pl.pallas_call(
        functools.partial(_moe_kernel, nf=nf),
        grid_spec=grid_spec,
        out_shape=jax.ShapeDtypeStruct((MOE_ROWS, D), BF16),
        compiler_params=_cparams(("arbitrary", "arbitrary"), est),
        name="moe_experts",
    )(tile_expert, n_tiles, h_sorted, row_w, wg, wu, wd)


def _final_kernel(x_ref, mod_ref, y0_ref, y1_ref, g_ref, o_ref):
    x = x_ref[...] + mod_ref[5:6, :] * (y0_ref[...].astype(F32) + y1_ref[...].astype(F32))
    o_ref[...] = x * lax.rsqrt(jnp.mean(x * x, -1, keepdims=True) + EPS) * g_ref[...]


def _final(X, mod, y0, y1, gain):
    tm = 512
    est = 2 * (2 * tm * D * 4 + 2 * tm * D * 2) + 3 * tm * D * 4
    return pl.pallas_call(
        _final_kernel,
        grid=(MX // tm,),
        in_specs=[
            pl.BlockSpec((tm, D), lambda i: (i, 0)),
            _mod_spec(tm),
            pl.BlockSpec((tm, D), lambda i: (i, 0)),
            pl.BlockSpec((tm, D), lambda i: (i, 0)),
            pl.BlockSpec((1, D), lambda i: (0, 0)),
        ],
        out_specs=pl.BlockSpec((tm, D), lambda i: (i, 0)),
        out_shape=jax.ShapeDtypeStruct((MX, D), F32),
        compiler_params=_cparams(("arbitrary",), est),
        name="final_norm",
    )(X, mod, y0, y1, gain)


def _moe_routing(route):
    tm = MOE_TM
    e_flat = route[:, 0:TOP_K].astype(jnp.int32).reshape(-1)
    w_flat = route[:, TOP_K:2 * TOP_K].reshape(-1)
    n = e_flat.shape[0]
    order = jnp.argsort(e_flat, stable=True).astype(jnp.int32)
    e_sorted = e_flat[order]
    counts = jnp.sum(e_flat[:, None] == jnp.arange(N_EXP, dtype=jnp.int32)[None, :], 0).astype(jnp.int32)
    padded = ((counts + tm - 1) // tm) * tm
    start = jnp.cumsum(counts) - counts
    pstart = jnp.cumsum(padded) - padded
    dest = pstart[e_sorted] + (jnp.arange(n, dtype=jnp.int32) - start[e_sorted])
    row_token = jnp.zeros((MOE_ROWS,), jnp.int32).at[dest].set(order // TOP_K)
    row_w = jnp.zeros((MOE_ROWS,), F32).at[dest].set(w_flat[order])
    slot_row = jnp.zeros((n,), jnp.int32).at[order].set(dest)
    pend = jnp.cumsum(padded)
    n_tiles = (pend[-1] // tm).astype(jnp.int32)
    tile_start = jnp.arange(MOE_TILES, dtype=jnp.int32) * tm
    tile_expert = jnp.sum(tile_start[:, None] >= pend[None, :], 1).astype(jnp.int32)
    last_expert = tile_expert[jnp.maximum(n_tiles - 1, 0)]
    tile_expert = jnp.where(tile_start < pend[-1], tile_expert, last_expert)
    return row_token, row_w.reshape(MOE_ROWS, 1), slot_row.reshape(MX, TOP_K), tile_expert, n_tiles.reshape(1)


def _dft_table(n):
    k = jnp.arange(n, dtype=jnp.int32)
    ang = ((k[:, None] * k[None, :]) % n).astype(F32) * (2.0 * np.pi / n)
    scale = n ** -0.5
    return jnp.concatenate([jnp.cos(ang) * scale, jnp.sin(ang) * scale], 1).astype(BF16)


def _channel_table():
    k = jnp.arange(F_GDIM, dtype=jnp.int32)
    ang = ((k[:, None] * k[None, :]) % F_GDIM).astype(F32) * (2.0 * np.pi / F_GDIM)
    scale = F_GDIM ** -0.5
    eye = jnp.eye(F_GROUPS, dtype=F32)
    return jnp.concatenate([jnp.kron(eye, jnp.cos(ang) * scale), jnp.kron(eye, -jnp.sin(ang) * scale)], 1).astype(BF16)


def _rope_tables():
    t = jnp.arange(T, dtype=jnp.int32)
    r = (t // GRID_W).astype(F32)[:, None]
    c = (t % GRID_W).astype(F32)[:, None]
    axis_dim = ROPE // 2
    inv = ROPE_BASE ** (-jnp.arange(0, axis_dim, 2, dtype=F32) / axis_dim)[None, :]
    cr, sr, cc, sc = jnp.cos(r * inv), jnp.sin(r * inv), jnp.cos(c * inv), jnp.sin(c * inv)
    cos32 = jnp.concatenate([cr, cr, cc, cc], 1)
    sin32 = jnp.concatenate([-sr, sr, -sc, sc], 1)
    ones = jnp.ones((T, NOPE), F32)
    zeros = jnp.zeros((T, NOPE), F32)
    tail = jnp.zeros((T, HP - NOPE - ROPE), F32)
    cos_x = jnp.concatenate([ones, cos32, tail], 1)
    sin_x = jnp.concatenate([zeros, sin32, tail], 1)
    cos_c = jnp.concatenate([jnp.ones((MC, NOPE + ROPE), F32), jnp.zeros((MC, HP - NOPE - ROPE), F32)], 1)
    cos_t = jnp.concatenate([jnp.tile(cos_x, (B, 1)), cos_c], 0)
    sin_t = jnp.concatenate([jnp.tile(sin_x, (B, 1)), jnp.zeros((MC, HP), F32)], 0)
    return cos_t, sin_t


_ROPE_SWAP = np.concatenate([np.arange(8, 16), np.arange(0, 8), np.arange(24, 32), np.arange(16, 24)])


def _layer_weights(w_in, w_qup, w_kvup, w_branch):
    sizes = (DN_W, DN_W, DN_W, DN_W, 2 * H_DN, 2 * H_DN, Q_RANK, KV_RANK, ROPE, F_W, 3 * D)
    offs = np.concatenate([[0], np.cumsum(sizes)])
    seg = [w_in[:, offs[i]:offs[i + 1]] for i in range(len(sizes))]
    dq, dk, dv, dz, da, db, cq, ckv, kpe, fo, gates = seg
    zc = lambda n: jnp.zeros((D, n), F32)
    w_main = jnp.concatenate([gates, dq, dk, dv, dz, fo, cq, zc(CQ_PAD - Q_RANK), ckv], 1).astype(BF16)
    w_small = jnp.concatenate([da, db, zc(NOPE - 4 * H_DN), kpe, zc(HP - NOPE - ROPE),
                               zc(NOPE), kpe[:, _ROPE_SWAP], zc(HP - NOPE - ROPE)], 1).astype(BF16)

    qh = w_qup.reshape(Q_RANK, H_MLA, NOPE + ROPE)
    zq = lambda n: jnp.zeros((Q_RANK, H_MLA, n), F32)
    wq1 = jnp.concatenate([qh, zq(HP - NOPE - ROPE)], 2).reshape(Q_RANK, H_MLA * HP)
    wq2 = jnp.concatenate([zq(NOPE), qh[:, :, NOPE:][:, :, _ROPE_SWAP], zq(HP - NOPE - ROPE)], 2).reshape(Q_RANK, H_MLA * HP)
    rowpad = jnp.zeros((CQ_PAD - Q_RANK, H_MLA * HP), F32)
    wq1 = jnp.concatenate([wq1, rowpad], 0).astype(BF16)
    wq2 = jnp.concatenate([wq2, rowpad], 0).astype(BF16)
    kvh = w_kvup.reshape(KV_RANK, H_MLA, NOPE + V_DIM)
    zk = jnp.zeros((KV_RANK, H_MLA, HP - NOPE), F32)
    wk = jnp.concatenate([kvh[:, :, :NOPE], zk], 2).reshape(KV_RANK, H_MLA * HP).astype(BF16)
    wv = jnp.concatenate([kvh[:, :, NOPE:], zk], 2).reshape(KV_RANK, H_MLA * HP).astype(BF16)

    wb = w_branch[1].reshape(H_MLA, V_DIM, D)
    wb = jnp.concatenate([wb, jnp.zeros((H_MLA, HP - V_DIM, D), F32)], 1).reshape(H_MLA * HP, D)
    return dict(w_main=w_main, w_small=w_small, wq1=wq1, wq2=wq2, wk=wk, wv=wv,
                wa=w_branch[0].astype(BF16), wb=wb.astype(BF16), wc=w_branch[2].astype(BF16))


def kernel(x, c, ctx, c_ctx, w_mod, b_mod, norm_mix, norm_ffn, w_in, dn_conv, dn_a_log, dn_dt_bias, dn_norm, mla_q_norm, mla_kv_norm, mla_w_qup, mla_w_kvup, w_branch, w_out, ffn_w_gate, ffn_w_up, ffn_w_down, moe_router, moe_w_gate, moe_w_up, moe_w_down, final_norm):
    X = jnp.concatenate([x.reshape(MX, D), ctx.reshape(MC, D)], 0)
    mods = _mod_vectors(c, c_ctx, w_mod, b_mod)
    cos_t, sin_t = _rope_tables()
    cs_x = _dft_table(T)
    cs_c = _dft_table(CTX)
    w_ab = _channel_table()

    out = None
    for l in range(DEPTH):
        last = l == DEPTH - 1
        mod = mods[l]
        lw = _layer_weights(w_in[l], mla_w_qup[l], mla_w_kvup[l], w_branch[l])
        u, usm = _inproj(X, norm_mix[l].reshape(1, D), mod, lw["w_main"], lw["w_small"])
        qkv, gb = _dnpre(u, usm, dn_conv[l], dn_a_log[l], dn_dt_bias[l])
        o0, o1 = _dnscan(qkv, gb)
        gq = jnp.concatenate([mla_q_norm[l], jnp.zeros((CQ_PAD - Q_RANK,), F32)]).reshape(1, CQ_PAD)
        q, k, v = _mlaproj(u, usm, cos_t, sin_t, gq, mla_kv_norm[l].reshape(1, KV_RANK),
                           lw["wq1"], lw["wq2"], lw["wk"], lw["wv"])
        yb = _attention(q, k, v, with_ctx_queries=not last)
        ab = _fourier_channels(u, w_ab)
        yc = _fourier_tokens(cs_x, ab, T, 0, 1024, 2048)
        if not last:
            yc = _fourier_tokens(cs_c, ab, CTX, MX, CTX, CTX, prev=yc)
        rows = MX if last else M
        X = _merge(X, mod, o0, o1, u, yb, yc, dn_norm[l].reshape(1, DK),
                   lw["wa"], lw["wb"], lw["wc"], w_out[l].astype(BF16), rows)
        e = l // 2
        if l % 2 == 0:
            X = _ffn(X, norm_ffn[l].reshape(1, D), mod, ffn_w_gate[e].astype(BF16),
                     ffn_w_up[e].astype(BF16), ffn_w_down[e].astype(BF16))
        else:
            wr = jnp.concatenate([moe_router[e], jnp.zeros((D, 128 - N_EXP), F32)], 1)
            h, route = _router(X, norm_ffn[l].reshape(1, D), mod, wr)
            row_token, row_w, slot_row, tile_expert, n_tiles = _moe_routing(route)
            h_sorted = jnp.take(h, row_token, axis=0)
            y = _moe_experts(tile_expert, n_tiles, h_sorted, row_w, moe_w_gate[e].astype(BF16),
                             moe_w_up[e].astype(BF16), moe_w_down[e].astype(BF16))
            y0 = jnp.take(y, slot_row[:, 0], axis=0)
            y1 = jnp.take(y, slot_row[:, 1], axis=0)
            out = _final(X, mod, y0, y1, final_norm.reshape(1, D))
    return out.reshape(B, T, D)
```

```python
import functools

import numpy as np
import jax
import jax.numpy as jnp
from jax import lax
from jax.experimental import pallas as pl
from jax.experimental.pallas import tpu as pltpu

F32 = jnp.float32
BF16 = jnp.bfloat16

D = 1024
B = 4
T = 4096
CTX = 256
DEPTH = 2
GRID_W = 64
H_DN = 4
DK = 128
DN_W = H_DN * DK
CONV_K = 5
CHUNK = 64
H_MLA = 8
Q_RANK = 384
KV_RANK = 256
NOPE = 64
ROPE = 32
V_DIM = 64
QK_SCALE = (NOPE + ROPE) ** -0.5
ROPE_BASE = 10000.0
F_GROUPS = 4
F_GDIM = 128
F_W = F_GROUPS * F_GDIM
D_FF = 2816
N_EXP = 8
TOP_K = 2
E_FF = 3584
EPS = 1e-6

MX = B * T
MC = B * CTX
M = MX + MC

HP = 128
C_GATES = 0
C_QKV = 3 * D
C_Z = C_QKV + 3 * DN_W
C_F = C_Z + DN_W
C_CQ = C_F + F_W
CQ_PAD = 512
C_CKV = C_CQ + CQ_PAD
N_MAIN = C_CKV + KV_RANK
N_SMALL = 256

V7X_VMEM_BYTES = 64 * 1024 * 1024
V7X_VMEM_USABLE = 56 * 1024 * 1024


def _cparams(sem, vmem_est_bytes):
    limit = int(min(V7X_VMEM_USABLE, max(32 * 1024 * 1024, 2 * vmem_est_bytes)))
    return pltpu.CompilerParams(dimension_semantics=sem, vmem_limit_bytes=limit)


def _norm_mod(x, gain, shift, scale):
    y = x * lax.rsqrt(jnp.mean(x * x, -1, keepdims=True) + EPS) * gain
    return y * (1.0 + scale) + shift


def _bdot(a, b):
    return jnp.dot(a, b, preferred_element_type=F32)


def _split3(v):
    hi = v.astype(BF16)
    r1 = v - hi.astype(F32)
    mid = r1.astype(BF16)
    lo = (r1 - mid.astype(F32)).astype(BF16)
    return hi, mid, lo


def _mod_kernel(a_ref, w_ref, b_ref, o_ref):
    a = a_ref[...]
    a = a * jax.nn.sigmoid(a)
    w = w_ref[...]
    a_hi = a.astype(BF16)
    a_lo = (a - a_hi.astype(F32)).astype(BF16)
    w_hi = w.astype(BF16)
    w_lo = (w - w_hi.astype(F32)).astype(BF16)
    o_ref[...] = _bdot(a_hi, w_hi) + _bdot(a_lo, w_hi) + _bdot(a_hi, w_lo) + b_ref[...]


def _mod_vectors(c, c_ctx, w_mod, b_mod):
    tn = 1536
    a = jnp.concatenate([c, c_ctx[None, :], jnp.zeros((3, D), F32)], 0)
    out = pl.pallas_call(
        _mod_kernel,
        grid=(DEPTH, 6 * D // tn),
        in_specs=[
            pl.BlockSpec((8, D), lambda l, j: (0, 0)),
            pl.BlockSpec((None, D, tn), lambda l, j: (l, 0, j)),
            pl.BlockSpec((None, 1, tn), lambda l, j: (l, 0, j)),
        ],
        out_specs=pl.BlockSpec((None, 8, tn), lambda l, j: (l, 0, j)),
        out_shape=jax.ShapeDtypeStruct((DEPTH, 8, 6 * D), F32),
        compiler_params=_cparams(("arbitrary", "arbitrary"), 2 * D * tn * 4 * 2),
        name="mod_vectors",
    )(a, w_mod, b_mod.reshape(DEPTH, 1, 6 * D))
    return out.reshape(DEPTH, 8, 6, D)[:, : B + 1]


def _mod_spec(tm):
    tpb = T // tm
    return pl.BlockSpec((None, 6, D), lambda i, *_: (jnp.minimum(i // tpb, B), 0, 0))


def _inproj_kernel(x_ref, g_ref, mod_ref, w_ref, ws_ref, o_ref, os_ref, h_ref):
    @pl.when(pl.program_id(1) == 0)
    def _():
        hb = _norm_mod(x_ref[...], g_ref[...], mod_ref[0:1, :], mod_ref[1:2, :]).astype(BF16)
        h_ref[...] = hb
        os_ref[...] = _bdot(hb, ws_ref[...])

    o_ref[...] = _bdot(h_ref[...], w_ref[...]).astype(o_ref.dtype)


def _inproj(X, gain, mod, w_main, w_small):
    tm, tn = 512, 1280
    est = 2 * (tm * D * 4 + D * tn * 2 + D * N_SMALL * 2 + tm * tn * 2 + tm * N_SMALL * 4) + tm * D * 2
    return pl.pallas_call(
        _inproj_kernel,
        grid=(M // tm, N_MAIN // tn),
        in_specs=[
            pl.BlockSpec((tm, D), lambda i, j: (i, 0)),
            pl.BlockSpec((1, D), lambda i, j: (0, 0)),
            _mod_spec(tm),
            pl.BlockSpec((D, tn), lambda i, j: (0, j)),
            pl.BlockSpec((D, N_SMALL), lambda i, j: (0, 0)),
        ],
        out_specs=[
            pl.BlockSpec((tm, tn), lambda i, j: (i, j)),
            pl.BlockSpec((tm, N_SMALL), lambda i, j: (i, 0)),
        ],
        out_shape=[jax.ShapeDtypeStruct((M, N_MAIN), BF16), jax.ShapeDtypeStruct((M, N_SMALL), F32)],
        scratch_shapes=[pltpu.VMEM((tm, D), BF16)],
        compiler_params=_cparams(("arbitrary", "arbitrary"), est),
        name="in_proj",
    )(X, gain, mod, w_main, w_small)


DN_TT = 256
DN_HALO = 16


def _dnpre_kernel(cur_ref, prev_ref, next_ref, sm_ref, cw_ref, al_ref, dtb_ref, qkv_ref, gb_ref):
    i = pl.program_id(0)
    tiles_per_seq = T // DN_TT
    is_ctx = i >= MX // DN_TT
    first = jnp.logical_or(is_ctx, i % tiles_per_seq == 0)
    last = jnp.logical_or(is_ctx, i % tiles_per_seq == tiles_per_seq - 1)
    prev = jnp.where(first, 0.0, prev_ref[...].astype(F32))
    nxt = jnp.where(last, 0.0, next_ref[...].astype(F32))
    xc = jnp.concatenate([prev, cur_ref[...].astype(F32), nxt], axis=0)
    n = xc.shape[0]
    half = CONV_K // 2
    acc = xc * cw_ref[half:half + 1, :]
    for j in range(CONV_K):
        if j != half:
            acc = acc + pltpu.roll(xc, (half - j) % n, 0) * cw_ref[j:j + 1, :]
    y = acc[DN_HALO:DN_HALO + DN_TT]
    y = y * jax.nn.sigmoid(y)
    for p in range(3 * H_DN):
        blk = y[:, p * DK:(p + 1) * DK]
        if p < 2 * H_DN:
            blk = blk * lax.rsqrt(jnp.sum(blk * blk, -1, keepdims=True) + EPS)
        qkv_ref[:, p * DK:(p + 1) * DK] = blk.astype(BF16)

    sm = sm_ref[...]
    lane = lax.broadcasted_iota(jnp.int32, sm.shape, 1)
    g = -jnp.exp(al_ref[...]) * jax.nn.softplus(sm + dtb_ref[...])
    gb_ref[...] = jnp.where(lane < 2 * H_DN, g, jax.nn.sigmoid(sm))


def _dnpre(u, usm, conv_w, a_log, dt_bias):
    tt = DN_TT
    nb16 = tt // DN_HALO
    qkv_cb = C_QKV // (3 * DN_W)
    pad = jnp.zeros((1, 128 - 2 * H_DN), F32)
    al = jnp.concatenate([a_log.reshape(1, 2 * H_DN), pad], 1)
    dtb = jnp.concatenate([dt_bias.reshape(1, 2 * H_DN), pad], 1)
    est = 2 * (tt * 1536 * 2 * 2 + tt * 128 * 4 * 2) + 6 * (tt + 32) * 1536 * 4
    return pl.pallas_call(
        _dnpre_kernel,
        grid=(M // tt,),
        in_specs=[
            pl.BlockSpec((tt, 3 * DN_W), lambda i: (i, qkv_cb)),
            pl.BlockSpec((DN_HALO, 3 * DN_W), lambda i: (jnp.maximum(i * nb16 - 1, 0), qkv_cb)),
            pl.BlockSpec((DN_HALO, 3 * DN_W), lambda i: (jnp.minimum((i + 1) * nb16, M // DN_HALO - 1), qkv_cb)),
            pl.BlockSpec((tt, 128), lambda i: (i, 0)),
            pl.BlockSpec((CONV_K, 3 * DN_W), lambda i: (0, 0)),
            pl.BlockSpec((1, 128), lambda i: (0, 0)),
            pl.BlockSpec((1, 128), lambda i: (0, 0)),
        ],
        out_specs=[
            pl.BlockSpec((tt, 3 * DN_W), lambda i: (_dn_group(i), 0)),
            pl.BlockSpec((tt, 128), lambda i: (_dn_group(i), 0)),
        ],
        out_shape=[jax.ShapeDtypeStruct((M, 3 * DN_W), BF16), jax.ShapeDtypeStruct((M, 128), F32)],
        compiler_params=_cparams(("arbitrary",), est),
        name="dn_pre",
    )(u, u, u, usm, conv_w, al, dtb)


DN_GC = DN_TT // CHUNK
DN_XG = T // DN_TT
DN_NB = 2


def _dnscan_kernel(qkv0_ref, qkv1_ref, gb0_ref, gb1_ref, o0_ref, o1_ref, s_ref):
    @pl.when(pl.program_id(1) == 0)
    def _():
        s_ref[...] = jnp.zeros_like(s_ref)

    def body(j, carry):
        _dnscan_chunk(j, qkv0_ref, qkv1_ref, gb0_ref, gb1_ref, o0_ref, o1_ref, s_ref)
        return carry

    lax.fori_loop(0, DN_GC, body, 0)


def _dnscan_chunk(j, qkv0_ref, qkv1_ref, gb0_ref, gb1_ref, o0_ref, o1_ref, s_ref):
    C = CHUNK
    W2 = 2 * DK
    ri = lax.broadcasted_iota(jnp.int32, (C, 2 * C), 0)
    li = lax.broadcasted_iota(jnp.int32, (C, 2 * C), 1)
    lm = li % C
    first = li < C
    firstw = lax.broadcasted_iota(jnp.int32, (C, W2), 1) < DK
    eye2 = (ri == lm).astype(F32)
    ii = lax.broadcasted_iota(jnp.int32, (C, C), 0)
    jj = lax.broadcasted_iota(jnp.int32, (C, C), 1)
    sr = lax.broadcasted_iota(jnp.int32, (W2, W2), 0) < DK
    sc = lax.broadcasted_iota(jnp.int32, (W2, W2), 1) < DK
    sdiag = sr == sc
    qscale = DK ** -0.5
    nt = (((1,), (1,)), ((), ()))
    tn = (((0,), (0,)), ((), ()))

    def bd(m2):
        return jnp.concatenate([jnp.where(first, m2, 0.0), jnp.where(first, 0.0, m2)], 0).astype(BF16)

    def bdw(m2):
        return jnp.concatenate([jnp.where(firstw, m2, 0.0), jnp.where(firstw, 0.0, m2)], 0).astype(BF16)

    ch = []
    for bb, d in [(bb, d) for bb in range(DN_NB) for d in range(2)]:
        rows = pl.ds(pl.multiple_of(bb * DN_TT + (j if d == 0 else DN_GC - 1 - j) * C, C), C)
        qkv_ref = (qkv0_ref, qkv1_ref)[d]
        gbv = (gb0_ref, gb1_ref)[d][rows, :]
        incl = (ri >= lm) if d == 0 else (ri <= lm)
        strict = (ri > lm) if d == 0 else (ri < lm)
        tri = ((ii >= jj) if d == 0 else (ii <= jj)).astype(BF16)
        g_hi, g_mid, g_lo = _split3(gbv)
        gc = _bdot(tri, g_hi) + _bdot(tri, g_mid) + _bdot(tri, g_lo)
        gct = jnp.concatenate([gc, gc], 0).T
        last_row = C - 1 if d == 0 else 0
        for p in range(H_DN // 2):
            la = d * H_DN + 2 * p
            lb = la + 1
            col2 = jnp.where(first, gc[:, la:la + 1], gc[:, lb:lb + 1])
            row2 = jnp.where(first[0:1], gct[la:la + 1, :], gct[lb:lb + 1, :])
            colw = jnp.where(firstw, gc[:, la:la + 1], gc[:, lb:lb + 1])
            betaw = jnp.where(firstw, gbv[:, 2 * H_DN + la:2 * H_DN + la + 1], gbv[:, 2 * H_DN + lb:2 * H_DN + lb + 1])
            gl_a = gc[last_row:last_row + 1, la:la + 1]
            gl_b = gc[last_row:last_row + 1, lb:lb + 1]
            k2 = qkv_ref[rows, DN_W + p * W2:DN_W + (p + 1) * W2].astype(F32)
            kb = k2 * betaw
            qs = qkv_ref[rows, p * W2:(p + 1) * W2].astype(F32) * qscale
            ecolw = jnp.exp(colw)
            vbeta = qkv_ref[rows, 2 * DN_W + p * W2:2 * DN_W + (p + 1) * W2].astype(F32) * betaw
            kbe = kb * ecolw
            ch.append(dict(
                d=d, p=p, strict=strict, rows=rows, si=(bb * 2 + d) * (H_DN // 2) + p,
                dec=jnp.exp(jnp.where(incl, col2 - row2, -jnp.inf)),
                lhs=jnp.concatenate([kb.astype(BF16), qs.astype(BF16)], 0),
                kbd=bdw(k2),
                qg=(qs * ecolw).astype(BF16),
                kd=(k2 * jnp.exp(jnp.where(firstw, gl_a, gl_b) - colw)).astype(BF16),
                rhs=jnp.concatenate([
                    jnp.concatenate([vbeta[:, 0:DK], kbe[:, 0:DK]], 1),
                    jnp.concatenate([vbeta[:, DK:W2], kbe[:, DK:W2]], 1)], 0).astype(BF16),
                gls=jnp.where(sr, jnp.exp(gl_a), jnp.exp(gl_b)),
            ))

    for c in ch:
        c["kq"] = lax.dot_general(c["lhs"], c["kbd"], nt, preferred_element_type=F32)
    for c in ch:
        kq = c["kq"]
        nmat = -jnp.where(c["strict"], kq[0:C] * c["dec"], 0.0)
        c["attn"] = (kq[C:2 * C] * c["dec"]).astype(BF16)
        c["x"] = eye2 + nmat
        c["nm"] = nmat
    for c in ch:
        c["pw"] = _bdot(c["nm"].astype(BF16), bd(c["nm"]))
    for it in range(5):
        for c in ch:
            c["pbd"] = bd(c["pw"])
        for c in ch:
            c["x"] = c["x"] + _bdot(c["x"].astype(BF16), c["pbd"])
        if it < 4:
            for c in ch:
                c["pw"] = _bdot(c["pw"].astype(BF16), c["pbd"])
    for c in ch:
        x = c["x"]
        xl = jnp.concatenate([jnp.where(first, x, 0.0), jnp.where(first, 0.0, x)], 0).astype(BF16)
        uw = _bdot(xl, c["rhs"])
        c["u"] = jnp.concatenate([uw[0:C, 0:DK], uw[C:2 * C, 0:DK]], 1)
        w = jnp.concatenate([uw[0:C, DK:W2], uw[C:2 * C, DK:W2]], 1)
        c["wq"] = jnp.concatenate([w.astype(BF16), c["qg"]], 0)
    for c in ch:
        c["s"] = s_ref[c["si"]]
        c["wqs"] = _bdot(c["wq"], c["s"].astype(BF16))
    for c in ch:
        c["vn"] = c["u"] - c["wqs"][0:C]
    for c in ch:
        o = c["wqs"][C:2 * C] + _bdot(c["attn"], bdw(c["vn"]))
        o_ref = (o0_ref, o1_ref)[c["d"]]
        o_ref[c["rows"], c["p"] * W2:(c["p"] + 1) * W2] = o.astype(o_ref.dtype)
    for c in ch:
        kv = lax.dot_general(c["kd"], c["vn"].astype(BF16), tn, preferred_element_type=F32)
        s_ref[c["si"]] = c["s"] * c["gls"] + jnp.where(sdiag, kv, 0.0)


def _dn_group(i):
    b = i // DN_XG
    g = i % DN_XG
    return jnp.where(i < MX // DN_TT, ((b // DN_NB) * DN_XG + g) * DN_NB + b % DN_NB, i)


def _dnscan(qkv, gb):
    cblk = MX // (DN_NB * DN_TT)
    rows = DN_NB * DN_TT

    def idx0(bp, s):
        return jnp.where(s == 0, cblk + bp, bp * DN_XG + s - 1)

    def idx1(bp, s):
        return jnp.where(s == 0, cblk + bp, bp * DN_XG + DN_XG - s)

    n_state = DN_NB * 2 * (H_DN // 2)
    est = 2 * 2 * (rows * 1536 * 2 + rows * 128 * 4 + rows * 512 * 4) + n_state * 4 * DK * DK * 4 + (4 << 20)
    return pl.pallas_call(
        _dnscan_kernel,
        grid=(B // DN_NB, 1 + DN_XG),
        in_specs=[
            pl.BlockSpec((rows, 3 * DN_W), lambda bp, s: (idx0(bp, s), 0)),
            pl.BlockSpec((rows, 3 * DN_W), lambda bp, s: (idx1(bp, s), 0)),
            pl.BlockSpec((rows, 128), lambda bp, s: (idx0(bp, s), 0)),
            pl.BlockSpec((rows, 128), lambda bp, s: (idx1(bp, s), 0)),
        ],
        out_specs=[
            pl.BlockSpec((rows, DN_W), lambda bp, s: (idx0(bp, s), 0)),
            pl.BlockSpec((rows, DN_W), lambda bp, s: (idx1(bp, s), 0)),
        ],
        out_shape=[jax.ShapeDtypeStruct((M, DN_W), F32), jax.ShapeDtypeStruct((M, DN_W), F32)],
        scratch_shapes=[pltpu.VMEM((n_state, 2 * DK, 2 * DK), F32)],
        compiler_params=_cparams(("arbitrary", "arbitrary"), est),
        name="dn_scan",
    )(qkv, qkv, gb, gb)


def _mlaproj_kernel(cq_ref, ckv_ref, sm_ref, cos_ref, sin_ref, gq_ref, gkv_ref,
                    wq1_ref, wq2_ref, wk_ref, wv_ref, q_ref, k_ref, v_ref):
    cq = cq_ref[...].astype(F32)
    hq = (cq * lax.rsqrt(jnp.sum(cq * cq, -1, keepdims=True) * (1.0 / Q_RANK) + EPS) * gq_ref[...]).astype(BF16)
    ckv = ckv_ref[...].astype(F32)
    hkv = (ckv * lax.rsqrt(jnp.mean(ckv * ckv, -1, keepdims=True) + EPS) * gkv_ref[...]).astype(BF16)
    q1 = _bdot(hq, wq1_ref[...])
    q2 = _bdot(hq, wq2_ref[...])
    kn = _bdot(hkv, wk_ref[...])
    vv = _bdot(hkv, wv_ref[...])
    cosq = cos_ref[...]
    sin = sin_ref[...]
    lane = lax.broadcasted_iota(jnp.int32, cosq.shape, 1)
    cosk = jnp.where(lane >= NOPE, cosq, 0.0)
    kpe = sm_ref[:, 0:HP] * cosk + sm_ref[:, HP:2 * HP] * sin
    for h in range(H_MLA):
        sl = slice(h * HP, (h + 1) * HP)
        q_ref[:, sl] = ((q1[:, sl] * cosq + q2[:, sl] * sin) * QK_SCALE).astype(BF16)
        k_ref[:, sl] = (kn[:, sl] + kpe).astype(BF16)
        v_ref[:, sl] = jnp.where(lane == V_DIM, 1.0, vv[:, sl]).astype(BF16)


def _mlaproj(u, usm, cos_t, sin_t, gq, gkv, wq1, wq2, wk, wv):
    tm = 256
    wide = H_MLA * HP
    est = 2 * (tm * (CQ_PAD + KV_RANK) * 2 + tm * 256 * 4 + 2 * tm * 128 * 4 + 2 * CQ_PAD * wide * 2
               + 2 * KV_RANK * wide * 2 + 3 * tm * wide * 2) + 6 * tm * wide * 4
    full = lambda shape: pl.BlockSpec(shape, lambda i: (0, 0))
    return pl.pallas_call(
        _mlaproj_kernel,
        grid=(M // tm,),
        in_specs=[
            pl.BlockSpec((tm, CQ_PAD), lambda i: (i, C_CQ // CQ_PAD)),
            pl.BlockSpec((tm, KV_RANK), lambda i: (i, C_CKV // KV_RANK)),
            pl.BlockSpec((tm, N_SMALL), lambda i: (i, 0)),
            pl.BlockSpec((tm, HP), lambda i: (i, 0)),
            pl.BlockSpec((tm, HP), lambda i: (i, 0)),
            full((1, CQ_PAD)), full((1, KV_RANK)),
            full((CQ_PAD, wide)), full((CQ_PAD, wide)), full((KV_RANK, wide)), full((KV_RANK, wide)),
        ],
        out_specs=[pl.BlockSpec((tm, wide), lambda i: (i, 0))] * 3,
        out_shape=[jax.ShapeDtypeStruct((M, wide), BF16)] * 3,
        compiler_params=_cparams(("arbitrary",), est),
        name="mla_proj",
    )(u, u, usm, cos_t, sin_t, gq, gkv, wq1, wq2, wk, wv)


ATT_TQ = 256
ATT_TK = 256


def _attn_kernel(q_ref, kx_ref, vx_ref, kc_ref, vc_ref, o_ref):
    qi = pl.program_id(2)
    q = q_ref[...]
    nt = (((1,), (1,)), ((), ()))

    def scores(k_blk):
        return lax.dot_general(q, k_blk, nt, preferred_element_type=F32)

    def update(carry, s, v_blk):
        m, acc = carry
        m_new = jnp.maximum(m, jnp.max(s, -1, keepdims=True))
        p = jnp.exp(s - m_new).astype(BF16)
        return m_new, jnp.exp(m - m_new) * acc + _bdot(p, v_blk)

    def finish(carry):
        acc = carry[1]
        o_ref[...] = (acc / acc[:, V_DIM:V_DIM + 1]).astype(o_ref.dtype)

    init = (jnp.full((ATT_TQ, 1), -jnp.inf, F32), jnp.zeros((ATT_TQ, HP), F32))

    @pl.when(qi < T // ATT_TQ)
    def _():
        nblk = T // ATT_TK
        s_next = scores(kx_ref[0:ATT_TK, :])
        carry = update(init, scores(kc_ref[...]), vc_ref[...])
        for j in range(nblk):
            s = s_next
            if j + 1 < nblk:
                s_next = scores(kx_ref[(j + 1) * ATT_TK:(j + 2) * ATT_TK, :])
            carry = update(carry, s, vx_ref[j * ATT_TK:(j + 1) * ATT_TK, :])
        finish(carry)

    @pl.when(qi >= T // ATT_TQ)
    def _():
        finish(update(init, scores(kc_ref[...]), vc_ref[...]))


def _attention(q, k, v, with_ctx_queries):
    tq = ATT_TQ
    nqx = T // tq
    nq = nqx + (1 if with_ctx_queries else 0)
    cblk = MX // CTX
    rows = M if with_ctx_queries else MX

    def qidx(b, h, i):
        return (jnp.where(i < nqx, b * nqx + i, MX // tq + b), h)

    est = 2 * (2 * T * HP * 2 + 2 * CTX * HP * 2 + 2 * tq * HP * 2) + 3 * tq * (T + CTX) * 4
    return pl.pallas_call(
        _attn_kernel,
        grid=(B, H_MLA, nq),
        in_specs=[
            pl.BlockSpec((tq, HP), qidx),
            pl.BlockSpec((T, HP), lambda b, h, i: (b, h)),
            pl.BlockSpec((T, HP), lambda b, h, i: (b, h)),
            pl.BlockSpec((CTX, HP), lambda b, h, i: (cblk + b, h)),
            pl.BlockSpec((CTX, HP), lambda b, h, i: (cblk + b, h)),
        ],
        out_specs=pl.BlockSpec((tq, HP), qidx),
        out_shape=jax.ShapeDtypeStruct((rows, H_MLA * HP), BF16),
        compiler_params=_cparams(("arbitrary", "arbitrary", "arbitrary"), est),
        name="mla_attention",
    )(q, k, v, k, v)


def _mm1_kernel(a_ref, b_ref, o_ref):
    o_ref[...] = _bdot(a_ref[...], b_ref[...]).astype(o_ref.dtype)


def _fourier_channels(u, w_ab):
    tm = 512
    est = 2 * (tm * F_W * 2 + F_W * 2 * F_W * 2 + tm * 2 * F_W * 2)
    return pl.pallas_call(
        _mm1_kernel,
        grid=(M // tm,),
        in_specs=[pl.BlockSpec((tm, F_W), lambda i: (i, C_F // F_W)),
                  pl.BlockSpec((F_W, 2 * F_W), lambda i: (0, 0))],
        out_specs=pl.BlockSpec((tm, 2 * F_W), lambda i: (i, 0)),
        out_shape=jax.ShapeDtypeStruct((M, 2 * F_W), BF16),
        compiler_params=_cparams(("arbitrary",), est),
        name="fourier_channels",
    )(u, w_ab)


def _dft_kernel(cs_ref, ab_ref, o_ref, acc_ref, *, nk):
    kk = pl.program_id(2)

    @pl.when(kk == 0)
    def _():
        acc_ref[...] = jnp.zeros_like(acc_ref)

    acc_ref[...] += _bdot(cs_ref[...], ab_ref[...])

    @pl.when(kk == nk - 1)
    def _():
        o_ref[...] = acc_ref[...].astype(o_ref.dtype)


def _fourier_tokens(cs, ab, seq, row0, tm, tk):
    nk = 2 * seq // tk
    kpb = seq // tk
    est = 2 * (tm * tk * 2 + tk * F_W * 2 + tm * F_W * 2) + tm * F_W * 4
    return pl.pallas_call(
        functools.partial(_dft_kernel, nk=nk),
        grid=(B, seq // tm, nk),
        in_specs=[
            pl.BlockSpec((tm, tk), lambda b, i, kk: (i, kk)),
            pl.BlockSpec((tk, F_W), lambda b, i, kk: (row0 // tk + b * kpb + kk % kpb, kk // kpb)),
        ],
        out_specs=pl.BlockSpec((tm, F_W), lambda b, i, kk: (b * (seq // tm) + i, 0)),
        out_shape=jax.ShapeDtypeStruct((B * seq, F_W), BF16),
        scratch_shapes=[pltpu.VMEM((tm, F_W), F32)],
        compiler_params=_cparams(("arbitrary", "arbitrary", "arbitrary"), est),
        name="fourier_tokens",
    )(cs, ab)


def _merge_kernel(x_ref, mod_ref, o0_ref, o1_ref, z_ref, yb_ref, ycx_ref, ycc_ref, gt_ref, gdn_ref,
                  wa_ref, wb_ref, wc_ref, wo_ref, out_ref, *, x_tiles):
    yc = jnp.where(pl.program_id(0) < x_tiles, ycx_ref[...], ycc_ref[...])
    o = o0_ref[...] + o1_ref[...]
    z = z_ref[...].astype(F32)
    gdn = gdn_ref[...]
    ya = []
    for h in range(H_DN):
        sl = slice(h * DK, (h + 1) * DK)
        oh = o[:, sl]
        zh = z[:, sl]
        yh = oh * lax.rsqrt(jnp.mean(oh * oh, -1, keepdims=True) + EPS) * gdn
        ya.append((yh * (zh * jax.nn.sigmoid(zh))).astype(BF16))
    ya = jnp.concatenate(ya, -1)
    s = jax.nn.sigmoid(gt_ref[:, 0:D].astype(F32)) * _bdot(ya, wa_ref[...])
    s = s + jax.nn.sigmoid(gt_ref[:, D:2 * D].astype(F32)) * _bdot(yb_ref[...], wb_ref[...])
    s = s + jax.nn.sigmoid(gt_ref[:, 2 * D:3 * D].astype(F32)) * _bdot(yc, wc_ref[...])
    mix = _bdot(s.astype(BF16), wo_ref[...])
    out_ref[...] = x_ref[...] + mod_ref[2:3, :] * mix


def _merge(X, mod, o0, o1, u, yb, ycx, ycc, gdn, wa, wb, wc, wo, rows):
    tm = 256
    x_tiles = MX // tm
    full = lambda shape: pl.BlockSpec(shape, lambda i: (0, 0))
    est = 2 * (2 * tm * D * 4 + 2 * tm * DN_W * 4 + tm * DN_W * 2 * 3 + tm * D * 2 + tm * 3 * D * 2
               + (2 * DN_W + 2 * D) * D * 2) + 6 * tm * D * 4
    return pl.pallas_call(
        functools.partial(_merge_kernel, x_tiles=x_tiles),
        grid=(rows // tm,),
        in_specs=[
            pl.BlockSpec((tm, D), lambda i: (i, 0)),
            _mod_spec(tm),
            pl.BlockSpec((tm, DN_W), lambda i: (_dn_group(i), 0)),
            pl.BlockSpec((tm, DN_W), lambda i: (_dn_group(i), 0)),
            pl.BlockSpec((tm, DN_W), lambda i: (i, C_Z // DN_W)),
            pl.BlockSpec((tm, H_MLA * HP), lambda i: (i, 0)),
            pl.BlockSpec((tm, F_W), lambda i: (jnp.minimum(i, x_tiles - 1), 0)),
            pl.BlockSpec((tm, F_W), lambda i: (jnp.maximum(i - x_tiles, 0), 0)),
            pl.BlockSpec((tm, 3 * D), lambda i: (i, 0)),
            full((1, DK)),
            full((DN_W, D)), full((H_MLA * HP, D)), full((F_W, D)), full((D, D)),
        ],
        out_specs=pl.BlockSpec((tm, D), lambda i: (i, 0)),
        out_shape=jax.ShapeDtypeStruct((rows, D), F32),
        compiler_params=_cparams(("arbitrary",), est),
        name="merge",
    )(X, mod, o0, o1, u, yb, ycx, ycc, u, gdn, wa, wb, wc, wo)


def _ffn_kernel(x_ref, g_ref, mod_ref, wg_ref, wu_ref, wd_ref, o_ref, h_ref, acc_ref, *, nf):
    f = pl.program_id(1)

    @pl.when(f == 0)
    def _():
        h_ref[...] = _norm_mod(x_ref[...], g_ref[...], mod_ref[3:4, :], mod_ref[4:5, :]).astype(BF16)
        acc_ref[...] = jnp.zeros_like(acc_ref)

    h = h_ref[...]
    a = _bdot(h, wg_ref[...])
    up = _bdot(h, wu_ref[...])
    act = (a * jax.nn.sigmoid(a) * up).astype(BF16)
    acc_ref[...] += _bdot(act, wd_ref[...])

    @pl.when(f == nf - 1)
    def _():
        o_ref[...] = x_ref[...] + mod_ref[5:6, :] * acc_ref[...]


def _ffn(X, gain, mod, wg, wu, wd):
    tm, tf = 512, 1408
    nf = D_FF // tf
    est = 2 * (2 * tm * D * 4 + 3 * D * tf * 2) + tm * D * 6 + 3 * tm * tf * 4
    return pl.pallas_call(
        functools.partial(_ffn_kernel, nf=nf),
        grid=(M // tm, nf),
        in_specs=[
            pl.BlockSpec((tm, D), lambda i, f: (i, 0)),
            pl.BlockSpec((1, D), lambda i, f: (0, 0)),
            _mod_spec(tm),
            pl.BlockSpec((D, tf), lambda i, f: (0, f)),
            pl.BlockSpec((D, tf), lambda i, f: (0, f)),
            pl.BlockSpec((tf, D), lambda i, f: (f, 0)),
        ],
        out_specs=pl.BlockSpec((tm, D), lambda i, f: (i, 0)),
        out_shape=jax.ShapeDtypeStruct((M, D), F32),
        scratch_shapes=[pltpu.VMEM((tm, D), BF16), pltpu.VMEM((tm, D), F32)],
        compiler_params=_cparams(("arbitrary", "arbitrary"), est),
        name="ffn_dense",
    )(X, gain, mod, wg, wu, wd)


def _router_kernel(x_ref, g_ref, mod_ref, wr_ref, h_ref, r_ref):
    h = _norm_mod(x_ref[...], g_ref[...], mod_ref[3:4, :], mod_ref[4:5, :])
    h_ref[...] = h
    h_hi = h.astype(BF16)
    h_lo = (h - h_hi.astype(F32)).astype(BF16)
    wr = wr_ref[...]
    w_hi = wr.astype(BF16)
    w_lo = (wr - w_hi.astype(F32)).astype(BF16)
    logits = _bdot(h_hi, w_hi) + _bdot(h_lo, w_hi) + _bdot(h_hi, w_lo)
    lane = lax.broadcasted_iota(jnp.int32, logits.shape, 1)
    logits = jnp.where(lane < N_EXP, logits, -jnp.inf)
    m1 = jnp.max(logits, -1, keepdims=True)
    i1 = jnp.min(jnp.where(logits == m1, lane, 128), -1, keepdims=True)
    rest = jnp.where(lane == i1, -jnp.inf, logits)
    m2 = jnp.max(rest, -1, keepdims=True)
    i2 = jnp.min(jnp.where(rest == m2, lane, 128), -1, keepdims=True)
    e2 = jnp.exp(m2 - m1)
    w1 = 1.0 / (1.0 + e2)
    w2 = e2 / (1.0 + e2)
    r = jnp.where(lane == 0, i1.astype(F32), 0.0)
    r = jnp.where(lane == 1, i2.astype(F32), r)
    r = jnp.where(lane == 2, w1, r)
    r = jnp.where(lane == 3, w2, r)
    r_ref[...] = r


def _router(X, gain, mod, wr_pad):
    tm = 512
    est = 2 * (tm * D * 4 + D * 128 * 4 + tm * D * 2 + tm * 128 * 4) + 4 * tm * D * 4
    return pl.pallas_call(
        _router_kernel,
        grid=(MX // tm,),
        in_specs=[
            pl.BlockSpec((tm, D), lambda i: (i, 0)),
            pl.BlockSpec((1, D), lambda i: (0, 0)),
            _mod_spec(tm),
            pl.BlockSpec((D, 128), lambda i: (0, 0)),
        ],
        out_specs=[pl.BlockSpec((tm, D), lambda i: (i, 0)), pl.BlockSpec((tm, 128), lambda i: (i, 0))],
        out_shape=[jax.ShapeDtypeStruct((MX, D), F32), jax.ShapeDtypeStruct((MX, 128), F32)],
        compiler_params=_cparams(("arbitrary",), est),
        name="moe_router",
    )(X, gain, mod, wr_pad)


GATHER_ROWS = 512


def _gather_kernel(tok_ref, src_hbm, dst_hbm, sem):
    base = pl.program_id(0) * GATHER_ROWS

    def row_copy(r):
        return pltpu.make_async_copy(src_hbm.at[pl.ds(tok_ref[base + r], 1)], dst_hbm.at[pl.ds(base + r, 1)], sem)

    def start(r, carry):
        row_copy(r).start()
        return carry

    def wait(r, carry):
        row_copy(r).wait()
        return carry

    lax.fori_loop(0, GATHER_ROWS, start, 0, unroll=8)
    lax.fori_loop(0, GATHER_ROWS, wait, 0, unroll=8)


def _gather_rows(src, row_token):
    grid_spec = pltpu.PrefetchScalarGridSpec(
        num_scalar_prefetch=1,
        grid=(MOE_ROWS // GATHER_ROWS,),
        in_specs=[pl.BlockSpec(memory_space=pl.ANY)],
        out_specs=pl.BlockSpec(memory_space=pl.ANY),
        scratch_shapes=[pltpu.SemaphoreType.DMA(())],
    )
    return pl.pallas_call(
        _gather_kernel,
        grid_spec=grid_spec,
        out_shape=jax.ShapeDtypeStruct((MOE_ROWS, D), src.dtype),
        compiler_params=pltpu.CompilerParams(dimension_semantics=("arbitrary",)),
        name="moe_gather",
    )(row_token, src)


MOE_TM = 512
MOE_TF = 1792
MOE_ROWS = MX * TOP_K + N_EXP * MOE_TM
MOE_TILES = MOE_ROWS // MOE_TM


def _moe_kernel(te_ref, nt_ref, h_ref, rw_ref, wg_ref, wu_ref, wd_ref, o_ref, hb_ref, acc_ref, *, nf):
    t = pl.program_id(0)
    f = pl.program_id(1)
    valid = t < nt_ref[0]

    @pl.when(jnp.logical_and(valid, f == 0))
    def _():
        hb_ref[...] = h_ref[...].astype(BF16)
        acc_ref[...] = jnp.zeros_like(acc_ref)

    @pl.when(valid)
    def _():
        h = hb_ref[...]
        a = _bdot(h, wg_ref[...])
        up = _bdot(h, wu_ref[...])
        act = (a * jax.nn.sigmoid(a) * up).astype(BF16)
        acc_ref[...] += _bdot(act, wd_ref[...])

    @pl.when(jnp.logical_and(valid, f == nf - 1))
    def _():
        o_ref[...] = (acc_ref[...] * rw_ref[...]).astype(o_ref.dtype)

    @pl.when(jnp.logical_and(jnp.logical_not(valid), f == nf - 1))
    def _():
        o_ref[...] = jnp.zeros_like(o_ref)


def _moe_experts(tile_expert, n_tiles, h_sorted, row_w, wg, wu, wd):
    tm, tf = MOE_TM, MOE_TF
    nf = E_FF // tf

    def fsel(t, f, nt):
        return jnp.where(t < nt[0], f, nf - 1)

    est = 2 * (tm * D * 4 + tm * 128 * 4 + 3 * D * tf * 2 + tm * D * 2) + tm * D * 6 + 3 * tm * tf * 4
    grid_spec = pltpu.PrefetchScalarGridSpec(
        num_scalar_prefetch=2,
        grid=(MOE_TILES, nf),
        in_specs=[
            pl.BlockSpec((tm, D), lambda t, f, te, nt: (t, 0)),
            pl.BlockSpec((tm, 1), lambda t, f, te, nt: (t, 0)),
            pl.BlockSpec((None, D, tf), lambda t, f, te, nt: (te[t], 0, fsel(t, f, nt))),
            pl.BlockSpec((None, D, tf), lambda t, f, te, nt: (te[t], 0, fsel(t, f, nt))),
            pl.BlockSpec((None, tf, D), lambda t, f, te, nt: (te[t], fsel(t, f, nt), 0)),
        ],
        out_specs=pl.BlockSpec((tm, D), lambda t, f, te, nt: (t, 0)),
        scratch_shapes=[pltpu.VMEM((tm, D), BF16), pltpu.VMEM((tm, D), F32)],
    )
    return pl.pallas_call(
        functools.partial(_moe_kernel, nf=nf),
        grid_spec=grid_spec,
        out_shape=jax.ShapeDtypeStruct((MOE_ROWS, D), BF16),
        compiler_params=_cparams(("arbitrary", "arbitrary"), est),
        name="moe_experts",
    )(tile_expert, n_tiles, h_sorted, row_w, wg, wu, wd)


def _final_kernel(x_ref, mod_ref, y0_ref, y1_ref, g_ref, o_ref):
    x = x_ref[...] + mod_ref[5:6, :] * (y0_ref[...].astype(F32) + y1_ref[...].astype(F32))
    o_ref[...] = x * lax.rsqrt(jnp.mean(x * x, -1, keepdims=True) + EPS) * g_ref[...]


def _final(X, mod, y0, y1, gain):
    tm = 512
    est = 2 * (2 * tm * D * 4 + 2 * tm * D * 2) + 3 * tm * D * 4
    return pl.pallas_call(
        _final_kernel,
        grid=(MX // tm,),
        in_specs=[
            pl.BlockSpec((tm, D), lambda i: (i, 0)),
            _mod_spec(tm),
            pl.BlockSpec((tm, D), lambda i: (i, 0)),
            pl.BlockSpec((tm, D), lambda i: (i, 0)),
            pl.BlockSpec((1, D), lambda i: (0, 0)),
        ],
        out_specs=pl.BlockSpec((tm, D), lambda i: (i, 0)),
        out_shape=jax.ShapeDtypeStruct((MX, D), F32),
        compiler_params=_cparams(("arbitrary",), est),
        name="final_norm",
    )(X, mod, y0, y1, gain)


def _moe_routing(route):
    tm = MOE_TM
    e_flat = route[:, 0:TOP_K].astype(jnp.int32).reshape(-1)
    w_flat = route[:, TOP_K:2 * TOP_K].reshape(-1)
    n = e_flat.shape[0]
    eids = jnp.arange(N_EXP, dtype=jnp.int32)[None, :]

    def pick(onehot, table):
        return jnp.sum(onehot * table[None, :], 1)

    order = jnp.argsort(e_flat, stable=True).astype(jnp.int32)
    inv_order = jnp.argsort(order).astype(jnp.int32)
    oh_f = (e_flat[:, None] == eids).astype(jnp.int32)
    counts = jnp.sum(oh_f, 0)
    padded = ((counts + tm - 1) // tm) * tm
    start = jnp.cumsum(counts) - counts
    pend = jnp.cumsum(padded)
    pstart = pend - padded
    slot_row = pick(oh_f, pstart - start) + inv_order

    n_tiles = (pend[-1] // tm).astype(jnp.int32)
    tile_start = jnp.arange(MOE_TILES, dtype=jnp.int32) * tm
    tile_expert = jnp.minimum(jnp.sum(tile_start[:, None] >= pend[None, :], 1), N_EXP - 1).astype(jnp.int32)
    rows = jnp.arange(MOE_ROWS, dtype=jnp.int32)
    oh_r = (jnp.repeat(tile_expert, tm)[:, None] == eids).astype(jnp.int32)
    rank_r = rows - pick(oh_r, pstart)
    valid = jnp.logical_and(rank_r < pick(oh_r, counts), rows < pend[-1])
    slot_r = order[jnp.clip(pick(oh_r, start) + rank_r, 0, n - 1)]
    row_token = jnp.where(valid, slot_r // TOP_K, 0)
    row_w = jnp.where(valid, w_flat[slot_r], 0.0)
    last_expert = jnp.sum(jnp.where(jnp.arange(MOE_TILES) == n_tiles - 1, tile_expert, 0))
    tile_expert = jnp.where(tile_start < pend[-1], tile_expert, last_expert)
    return row_token, row_w.reshape(MOE_ROWS, 1), slot_row.reshape(MX, TOP_K), tile_expert, n_tiles.reshape(1)


def _dft_table(n):
    split = 64 if n > 64 else 1
    t = jnp.arange(n, dtype=jnp.int32)[None, :]
    a = jnp.arange(n // split, dtype=jnp.int32)[:, None]
    b = jnp.arange(split, dtype=jnp.int32)[:, None]
    na = n // split
    ang_a = ((a * t) % na).astype(F32) * (2.0 * np.pi / na)
    ang_b = ((b * t) % n).astype(F32) * (2.0 * np.pi / n)
    scale = n ** -0.5
    ca, sa = jnp.cos(ang_a)[:, None, :], jnp.sin(ang_a)[:, None, :]
    cb, sb = (jnp.cos(ang_b) * scale)[None, :, :], (jnp.sin(ang_b) * scale)[None, :, :]
    c = (ca * cb - sa * sb).reshape(n, n)
    s = (sa * cb + ca * sb).reshape(n, n)
    return jnp.concatenate([c, s], 1).astype(BF16)


def _channel_table():
    k = jnp.arange(F_GDIM, dtype=jnp.int32)
    ang = ((k[:, None] * k[None, :]) % F_GDIM).astype(F32) * (2.0 * np.pi / F_GDIM)
    scale = F_GDIM ** -0.5
    eye = jnp.eye(F_GROUPS, dtype=F32)
    return jnp.concatenate([jnp.kron(eye, jnp.cos(ang) * scale), jnp.kron(eye, -jnp.sin(ang) * scale)], 1).astype(BF16)


def _rope_tables():
    t = jnp.arange(T, dtype=jnp.int32)
    r = (t // GRID_W).astype(F32)[:, None]
    c = (t % GRID_W).astype(F32)[:, None]
    axis_dim = ROPE // 2
    inv = ROPE_BASE ** (-jnp.arange(0, axis_dim, 2, dtype=F32) / axis_dim)[None, :]
    cr, sr, cc, sc = jnp.cos(r * inv), jnp.sin(r * inv), jnp.cos(c * inv), jnp.sin(c * inv)
    cos32 = jnp.concatenate([cr, cr, cc, cc], 1)
    sin32 = jnp.concatenate([-sr, sr, -sc, sc], 1)
    ones = jnp.ones((T, NOPE), F32)
    zeros = jnp.zeros((T, NOPE), F32)
    tail = jnp.zeros((T, HP - NOPE - ROPE), F32)
    cos_x = jnp.concatenate([ones, cos32, tail], 1)
    sin_x = jnp.concatenate([zeros, sin32, tail], 1)
    cos_c = jnp.concatenate([jnp.ones((MC, NOPE + ROPE), F32), jnp.zeros((MC, HP - NOPE - ROPE), F32)], 1)
    cos_t = jnp.concatenate([jnp.tile(cos_x, (B, 1)), cos_c], 0)
    sin_t = jnp.concatenate([jnp.tile(sin_x, (B, 1)), jnp.zeros((MC, HP), F32)], 0)
    return cos_t, sin_t


_ROPE_SWAP = np.concatenate([np.arange(8, 16), np.arange(0, 8), np.arange(24, 32), np.arange(16, 24)])


def _layer_weights(w_in, w_qup, w_kvup, w_branch):
    sizes = (DN_W, DN_W, DN_W, DN_W, 2 * H_DN, 2 * H_DN, Q_RANK, KV_RANK, ROPE, F_W, 3 * D)
    offs = np.concatenate([[0], np.cumsum(sizes)])
    seg = [w_in[:, offs[i]:offs[i + 1]] for i in range(len(sizes))]
    dq, dk, dv, dz, da, db, cq, ckv, kpe, fo, gates = seg
    zc = lambda n: jnp.zeros((D, n), F32)
    w_main = jnp.concatenate([gates, dq, dk, dv, dz, fo, cq, zc(CQ_PAD - Q_RANK), ckv], 1).astype(BF16)
    w_small = jnp.concatenate([da, db, zc(NOPE - 4 * H_DN), kpe, zc(HP - NOPE - ROPE),
                               zc(NOPE), kpe[:, _ROPE_SWAP], zc(HP - NOPE - ROPE)], 1).astype(BF16)

    qh = w_qup.reshape(Q_RANK, H_MLA, NOPE + ROPE)
    zq = lambda n: jnp.zeros((Q_RANK, H_MLA, n), F32)
    wq1 = jnp.concatenate([qh, zq(HP - NOPE - ROPE)], 2).reshape(Q_RANK, H_MLA * HP)
    wq2 = jnp.concatenate([zq(NOPE), qh[:, :, NOPE:][:, :, _ROPE_SWAP], zq(HP - NOPE - ROPE)], 2).reshape(Q_RANK, H_MLA * HP)
    rowpad = jnp.zeros((CQ_PAD - Q_RANK, H_MLA * HP), F32)
    wq1 = jnp.concatenate([wq1, rowpad], 0).astype(BF16)
    wq2 = jnp.concatenate([wq2, rowpad], 0).astype(BF16)
    kvh = w_kvup.reshape(KV_RANK, H_MLA, NOPE + V_DIM)
    zk = jnp.zeros((KV_RANK, H_MLA, HP - NOPE), F32)
    wk = jnp.concatenate([kvh[:, :, :NOPE], zk], 2).reshape(KV_RANK, H_MLA * HP).astype(BF16)
    wv = jnp.concatenate([kvh[:, :, NOPE:], zk], 2).reshape(KV_RANK, H_MLA * HP).astype(BF16)

    wb = w_branch[1].reshape(H_MLA, V_DIM, D)
    wb = jnp.concatenate([wb, jnp.zeros((H_MLA, HP - V_DIM, D), F32)], 1).reshape(H_MLA * HP, D)
    return dict(w_main=w_main, w_small=w_small, wq1=wq1, wq2=wq2, wk=wk, wv=wv,
                wa=w_branch[0].astype(BF16), wb=wb.astype(BF16), wc=w_branch[2].astype(BF16))


def kernel(x, c, ctx, c_ctx, w_mod, b_mod, norm_mix, norm_ffn, w_in, dn_conv, dn_a_log, dn_dt_bias, dn_norm, mla_q_norm, mla_kv_norm, mla_w_qup, mla_w_kvup, w_branch, w_out, ffn_w_gate, ffn_w_up, ffn_w_down, moe_router, moe_w_gate, moe_w_up, moe_w_down, final_norm):
    X = jnp.concatenate([x.reshape(MX, D), ctx.reshape(MC, D)], 0)
    mods = _mod_vectors(c, c_ctx, w_mod, b_mod)
    cos_t, sin_t = _rope_tables()
    cs_x = _dft_table(T)
    cs_c = _dft_table(CTX)
    w_ab = _channel_table()

    out = None
    for l in range(DEPTH):
        last = l == DEPTH - 1
        mod = mods[l]
        lw = _layer_weights(w_in[l], mla_w_qup[l], mla_w_kvup[l], w_branch[l])
        u, usm = _inproj(X, norm_mix[l].reshape(1, D), mod, lw["w_main"], lw["w_small"])
        qkv, gb = _dnpre(u, usm, dn_conv[l], dn_a_log[l], dn_dt_bias[l])
        o0, o1 = _dnscan(qkv, gb)
        gq = jnp.concatenate([mla_q_norm[l], jnp.zeros((CQ_PAD - Q_RANK,), F32)]).reshape(1, CQ_PAD)
        q, k, v = _mlaproj(u, usm, cos_t, sin_t, gq, mla_kv_norm[l].reshape(1, KV_RANK),
                           lw["wq1"], lw["wq2"], lw["wk"], lw["wv"])
        yb = _attention(q, k, v, with_ctx_queries=not last)
        ab = _fourier_channels(u, w_ab)
        ycx = _fourier_tokens(cs_x, ab, T, 0, 1024, 2048)
        ycc = ycx if last else _fourier_tokens(cs_c, ab, CTX, MX, CTX, CTX)
        rows = MX if last else M
        X = _merge(X, mod, o0, o1, u, yb, ycx, ycc, dn_norm[l].reshape(1, DK),
                   lw["wa"], lw["wb"], lw["wc"], w_out[l].astype(BF16), rows)
        e = l // 2
        if l % 2 == 0:
            X = _ffn(X, norm_ffn[l].reshape(1, D), mod, ffn_w_gate[e].astype(BF16),
                     ffn_w_up[e].astype(BF16), ffn_w_down[e].astype(BF16))
        else:
            wr = jnp.concatenate([moe_router[e], jnp.zeros((D, 128 - N_EXP), F32)], 1)
            h, route = _router(X, norm_ffn[l].reshape(1, D), mod, wr)
            row_token, row_w, slot_row, tile_expert, n_tiles = _moe_routing(route)
            h_sorted = _gather_rows(h, row_token)
            y = _moe_experts(tile_expert, n_tiles, h_sorted, row_w, moe_w_gate[e].astype(BF16),
                             moe_w_up[e].astype(BF16), moe_w_down[e].astype(BF16))
            y0 = jnp.take(y, slot_row[:, 0], axis=0)
            y1 = jnp.take(y, slot_row[:, 1], axis=0)
            out = _final(X, mod, y0, y1, final_norm.reshape(1, D))
    return out.reshape(B, T, D)
```

```python
import functools

import numpy as np
import jax
import jax.numpy as jnp
from jax import lax
from jax.experimental import pallas as pl
from jax.experimental.pallas import tpu as pltpu

F32 = jnp.float32
BF16 = jnp.bfloat16

D = 1024
B = 4
T = 4096
CTX = 256
DEPTH = 2
GRID_W = 64
H_DN = 4
DK = 128
DN_W = H_DN * DK
CONV_K = 5
CHUNK = 64
H_MLA = 8
Q_RANK = 384
KV_RANK = 256
NOPE = 64
ROPE = 32
V_DIM = 64
QK_SCALE = (NOPE + ROPE) ** -0.5
ROPE_BASE = 10000.0
F_GROUPS = 4
F_GDIM = 128
F_W = F_GROUPS * F_GDIM
D_FF = 2816
N_EXP = 8
TOP_K = 2
E_FF = 3584
EPS = 1e-6

MX = B * T
MC = B * CTX
M = MX + MC

HP = 128
C_GATES = 0
C_QKV = 3 * D
C_Z = C_QKV + 3 * DN_W
C_F = C_Z + DN_W
C_CQ = C_F + F_W
CQ_PAD = 512
C_CKV = C_CQ + CQ_PAD
N_MAIN = C_CKV + KV_RANK
N_SMALL = 256

V7X_VMEM_BYTES = 64 * 1024 * 1024
V7X_VMEM_USABLE = 56 * 1024 * 1024


def _cparams(sem, vmem_est_bytes):
    limit = int(min(V7X_VMEM_USABLE, max(32 * 1024 * 1024, 2 * vmem_est_bytes)))
    return pltpu.CompilerParams(dimension_semantics=sem, vmem_limit_bytes=limit)


def _norm_mod(x, gain, shift, scale):
    y = x * lax.rsqrt(jnp.mean(x * x, -1, keepdims=True) + EPS) * gain
    return y * (1.0 + scale) + shift


def _bdot(a, b):
    return jnp.dot(a, b, preferred_element_type=F32)


def _split3(v):
    hi = v.astype(BF16)
    r1 = v - hi.astype(F32)
    mid = r1.astype(BF16)
    lo = (r1 - mid.astype(F32)).astype(BF16)
    return hi, mid, lo


def _mod_kernel(a_ref, w_ref, b_ref, o_ref):
    a = a_ref[...]
    a = a * jax.nn.sigmoid(a)
    w = w_ref[...]
    a_hi = a.astype(BF16)
    a_lo = (a - a_hi.astype(F32)).astype(BF16)
    w_hi = w.astype(BF16)
    w_lo = (w - w_hi.astype(F32)).astype(BF16)
    o_ref[...] = _bdot(a_hi, w_hi) + _bdot(a_lo, w_hi) + _bdot(a_hi, w_lo) + b_ref[...]


def _mod_vectors(c, c_ctx, w_mod, b_mod):
    tn = 1536
    a = jnp.concatenate([c, c_ctx[None, :], jnp.zeros((3, D), F32)], 0)
    out = pl.pallas_call(
        _mod_kernel,
        grid=(DEPTH, 6 * D // tn),
        in_specs=[
            pl.BlockSpec((8, D), lambda l, j: (0, 0)),
            pl.BlockSpec((None, D, tn), lambda l, j: (l, 0, j)),
            pl.BlockSpec((None, 1, tn), lambda l, j: (l, 0, j)),
        ],
        out_specs=pl.BlockSpec((None, 8, tn), lambda l, j: (l, 0, j)),
        out_shape=jax.ShapeDtypeStruct((DEPTH, 8, 6 * D), F32),
        compiler_params=_cparams(("arbitrary", "arbitrary"), 2 * D * tn * 4 * 2),
        name="mod_vectors",
    )(a, w_mod, b_mod.reshape(DEPTH, 1, 6 * D))
    return out.reshape(DEPTH, 8, 6, D)[:, : B + 1]


def _mod_spec(tm):
    tpb = T // tm
    return pl.BlockSpec((None, 6, D), lambda i, *_: (jnp.minimum(i // tpb, B), 0, 0))


def _inproj_kernel(x_ref, g_ref, mod_ref, w_ref, ws_ref, o_ref, os_ref, h_ref):
    @pl.when(pl.program_id(1) == 0)
    def _():
        hb = _norm_mod(x_ref[...], g_ref[...], mod_ref[0:1, :], mod_ref[1:2, :]).astype(BF16)
        h_ref[...] = hb
        os_ref[...] = _bdot(hb, ws_ref[...])

    o_ref[...] = _bdot(h_ref[...], w_ref[...]).astype(o_ref.dtype)


def _inproj(X, gain, mod, w_main, w_small):
    tm, tn = 1024, 3200
    est = 2 * (tm * D * 4 + D * tn * 2 + D * N_SMALL * 2 + tm * tn * 2 + tm * N_SMALL * 4) + tm * D * 2
    return pl.pallas_call(
        _inproj_kernel,
        grid=(M // tm, N_MAIN // tn),
        in_specs=[
            pl.BlockSpec((tm, D), lambda i, j: (i, 0)),
            pl.BlockSpec((1, D), lambda i, j: (0, 0)),
            _mod_spec(tm),
            pl.BlockSpec((D, tn), lambda i, j: (0, j)),
            pl.BlockSpec((D, N_SMALL), lambda i, j: (0, 0)),
        ],
        out_specs=[
            pl.BlockSpec((tm, tn), lambda i, j: (i, j)),
            pl.BlockSpec((tm, N_SMALL), lambda i, j: (i, 0)),
        ],
        out_shape=[jax.ShapeDtypeStruct((M, N_MAIN), BF16), jax.ShapeDtypeStruct((M, N_SMALL), F32)],
        scratch_shapes=[pltpu.VMEM((tm, D), BF16)],
        compiler_params=_cparams(("arbitrary", "arbitrary"), est),
        name="in_proj",
    )(X, gain, mod, w_main, w_small)


DN_TT = 256
DN_HALO = 16


def _dnpre_kernel(cur_ref, prev_ref, next_ref, sm_ref, cw_ref, al_ref, dtb_ref, qkv_ref, gb_ref):
    i = pl.program_id(0)
    tiles_per_seq = T // DN_TT
    is_ctx = i >= MX // DN_TT
    first = jnp.logical_or(is_ctx, i % tiles_per_seq == 0)
    last = jnp.logical_or(is_ctx, i % tiles_per_seq == tiles_per_seq - 1)
    prev = jnp.where(first, 0.0, prev_ref[...].astype(F32))
    nxt = jnp.where(last, 0.0, next_ref[...].astype(F32))
    xc = jnp.concatenate([prev, cur_ref[...].astype(F32), nxt], axis=0)
    n = xc.shape[0]
    half = CONV_K // 2
    acc = xc * cw_ref[half:half + 1, :]
    for j in range(CONV_K):
        if j != half:
            acc = acc + pltpu.roll(xc, (half - j) % n, 0) * cw_ref[j:j + 1, :]
    y = acc[DN_HALO:DN_HALO + DN_TT]
    y = y * jax.nn.sigmoid(y)
    for p in range(3 * H_DN):
        blk = y[:, p * DK:(p + 1) * DK]
        if p < 2 * H_DN:
            blk = blk * lax.rsqrt(jnp.sum(blk * blk, -1, keepdims=True) + EPS)
        qkv_ref[:, p * DK:(p + 1) * DK] = blk.astype(BF16)

    sm = sm_ref[...]
    lane = lax.broadcasted_iota(jnp.int32, sm.shape, 1)
    g = -jnp.exp(al_ref[...]) * jax.nn.softplus(sm + dtb_ref[...])
    gb_ref[...] = jnp.where(lane < 2 * H_DN, g, jax.nn.sigmoid(sm))


def _dnpre(u, usm, conv_w, a_log, dt_bias):
    tt = DN_TT
    nb16 = tt // DN_HALO
    qkv_cb = C_QKV // (3 * DN_W)
    pad = jnp.zeros((1, 128 - 2 * H_DN), F32)
    al = jnp.concatenate([a_log.reshape(1, 2 * H_DN), pad], 1)
    dtb = jnp.concatenate([dt_bias.reshape(1, 2 * H_DN), pad], 1)
    est = 2 * (tt * 1536 * 2 * 2 + tt * 128 * 4 * 2) + 6 * (tt + 32) * 1536 * 4
    return pl.pallas_call(
        _dnpre_kernel,
        grid=(M // tt,),
        in_specs=[
            pl.BlockSpec((tt, 3 * DN_W), lambda i: (i, qkv_cb)),
            pl.BlockSpec((DN_HALO, 3 * DN_W), lambda i: (jnp.maximum(i * nb16 - 1, 0), qkv_cb)),
            pl.BlockSpec((DN_HALO, 3 * DN_W), lambda i: (jnp.minimum((i + 1) * nb16, M // DN_HALO - 1), qkv_cb)),
            pl.BlockSpec((tt, 128), lambda i: (i, 0)),
            pl.BlockSpec((CONV_K, 3 * DN_W), lambda i: (0, 0)),
            pl.BlockSpec((1, 128), lambda i: (0, 0)),
            pl.BlockSpec((1, 128), lambda i: (0, 0)),
        ],
        out_specs=[
            pl.BlockSpec((tt, 3 * DN_W), lambda i: (_dn_group(i), 0)),
            pl.BlockSpec((tt, 128), lambda i: (_dn_group(i), 0)),
        ],
        out_shape=[jax.ShapeDtypeStruct((M, 3 * DN_W), BF16), jax.ShapeDtypeStruct((M, 128), F32)],
        compiler_params=_cparams(("arbitrary",), est),
        name="dn_pre",
    )(u, u, u, usm, conv_w, al, dtb)


DN_GC = DN_TT // CHUNK
DN_XG = T // DN_TT
DN_NB = 4


def _dnscan_kernel(qkv0_ref, qkv1_ref, gb0_ref, gb1_ref, o0_ref, o1_ref, s_ref):
    @pl.when(pl.program_id(1) == 0)
    def _():
        s_ref[...] = jnp.zeros_like(s_ref)

    def body(j, carry):
        _dnscan_chunk(j, qkv0_ref, qkv1_ref, gb0_ref, gb1_ref, o0_ref, o1_ref, s_ref)
        return carry

    lax.fori_loop(0, DN_GC, body, 0)


def _dnscan_chunk(j, qkv0_ref, qkv1_ref, gb0_ref, gb1_ref, o0_ref, o1_ref, s_ref):
    C = CHUNK
    W2 = 2 * DK
    ri = lax.broadcasted_iota(jnp.int32, (C, 2 * C), 0)
    li = lax.broadcasted_iota(jnp.int32, (C, 2 * C), 1)
    lm = li % C
    first = li < C
    firstw = lax.broadcasted_iota(jnp.int32, (C, W2), 1) < DK
    eye2 = (ri == lm).astype(F32)
    ii = lax.broadcasted_iota(jnp.int32, (C, C), 0)
    jj = lax.broadcasted_iota(jnp.int32, (C, C), 1)
    sr = lax.broadcasted_iota(jnp.int32, (W2, W2), 0) < DK
    sc = lax.broadcasted_iota(jnp.int32, (W2, W2), 1) < DK
    sdiag = sr == sc
    qscale = DK ** -0.5
    nt = (((1,), (1,)), ((), ()))
    tn = (((0,), (0,)), ((), ()))

    def bd(m2):
        return jnp.concatenate([jnp.where(first, m2, 0.0), jnp.where(first, 0.0, m2)], 0).astype(BF16)

    def bdw(m2):
        return jnp.concatenate([jnp.where(firstw, m2, 0.0), jnp.where(firstw, 0.0, m2)], 0).astype(BF16)

    ch = []
    for bb, d in [(bb, d) for bb in range(DN_NB) for d in range(2)]:
        rows = pl.ds(pl.multiple_of(bb * DN_TT + (j if d == 0 else DN_GC - 1 - j) * C, C), C)
        qkv_ref = (qkv0_ref, qkv1_ref)[d]
        gbv = (gb0_ref, gb1_ref)[d][rows, :]
        incl = (ri >= lm) if d == 0 else (ri <= lm)
        strict = (ri > lm) if d == 0 else (ri < lm)
        tri = ((ii >= jj) if d == 0 else (ii <= jj)).astype(BF16)
        g_hi, g_mid, g_lo = _split3(gbv)
        gc = _bdot(tri, g_hi) + _bdot(tri, g_mid) + _bdot(tri, g_lo)
        gct = jnp.concatenate([gc, gc], 0).T
        last_row = C - 1 if d == 0 else 0
        for p in range(H_DN // 2):
            la = d * H_DN + 2 * p
            lb = la + 1
            col2 = jnp.where(first, gc[:, la:la + 1], gc[:, lb:lb + 1])
            row2 = jnp.where(first[0:1], gct[la:la + 1, :], gct[lb:lb + 1, :])
            colw = jnp.where(firstw, gc[:, la:la + 1], gc[:, lb:lb + 1])
            betaw = jnp.where(firstw, gbv[:, 2 * H_DN + la:2 * H_DN + la + 1], gbv[:, 2 * H_DN + lb:2 * H_DN + lb + 1])
            gl_a = gc[last_row:last_row + 1, la:la + 1]
            gl_b = gc[last_row:last_row + 1, lb:lb + 1]
            k2 = qkv_ref[rows, DN_W + p * W2:DN_W + (p + 1) * W2].astype(F32)
            kb = k2 * betaw
            qs = qkv_ref[rows, p * W2:(p + 1) * W2].astype(F32) * qscale
            ecolw = jnp.exp(colw)
            vbeta = qkv_ref[rows, 2 * DN_W + p * W2:2 * DN_W + (p + 1) * W2].astype(F32) * betaw
            kbe = kb * ecolw
            ch.append(dict(
                d=d, p=p, strict=strict, rows=rows, si=(bb * 2 + d) * (H_DN // 2) + p,
                dec=jnp.exp(jnp.where(incl, col2 - row2, -jnp.inf)),
                lhs=jnp.concatenate([kb.astype(BF16), qs.astype(BF16)], 0),
                kbd=bdw(k2),
                qg=(qs * ecolw).astype(BF16),
                kd=(k2 * jnp.exp(jnp.where(firstw, gl_a, gl_b) - colw)).astype(BF16),
                rhs=jnp.concatenate([
                    jnp.concatenate([vbeta[:, 0:DK], kbe[:, 0:DK]], 1),
                    jnp.concatenate([vbeta[:, DK:W2], kbe[:, DK:W2]], 1)], 0).astype(BF16),
                gls=jnp.where(sr, jnp.exp(gl_a), jnp.exp(gl_b)),
            ))

    for c in ch:
        c["kq"] = lax.dot_general(c["lhs"], c["kbd"], nt, preferred_element_type=F32)
    for c in ch:
        kq = c["kq"]
        nmat = -jnp.where(c["strict"], kq[0:C] * c["dec"], 0.0)
        c["attn"] = (kq[C:2 * C] * c["dec"]).astype(BF16)
        c["x"] = eye2 + nmat
        c["nm"] = nmat
    for c in ch:
        c["pw"] = _bdot(c["nm"].astype(BF16), bd(c["nm"]))
    for it in range(5):
        for c in ch:
            c["pbd"] = bd(c["pw"])
        for c in ch:
            c["x"] = c["x"] + _bdot(c["x"].astype(BF16), c["pbd"])
        if it < 4:
            for c in ch:
                c["pw"] = _bdot(c["pw"].astype(BF16), c["pbd"])
    for c in ch:
        x = c["x"]
        xl = jnp.concatenate([jnp.where(first, x, 0.0), jnp.where(first, 0.0, x)], 0).astype(BF16)
        uw = _bdot(xl, c["rhs"])
        c["u"] = jnp.concatenate([uw[0:C, 0:DK], uw[C:2 * C, 0:DK]], 1)
        w = jnp.concatenate([uw[0:C, DK:W2], uw[C:2 * C, DK:W2]], 1)
        c["wq"] = jnp.concatenate([w.astype(BF16), c["qg"]], 0)
    for c in ch:
        c["s"] = s_ref[c["si"]]
        c["wqs"] = _bdot(c["wq"], c["s"].astype(BF16))
    for c in ch:
        c["vn"] = c["u"] - c["wqs"][0:C]
    for c in ch:
        o = c["wqs"][C:2 * C] + _bdot(c["attn"], bdw(c["vn"]))
        o_ref = (o0_ref, o1_ref)[c["d"]]
        o_ref[c["rows"], c["p"] * W2:(c["p"] + 1) * W2] = o.astype(o_ref.dtype)
    for c in ch:
        kv = lax.dot_general(c["kd"], c["vn"].astype(BF16), tn, preferred_element_type=F32)
        s_ref[c["si"]] = c["s"] * c["gls"] + jnp.where(sdiag, kv, 0.0)


def _dn_group(i):
    b = i // DN_XG
    g = i % DN_XG
    return jnp.where(i < MX // DN_TT, ((b // DN_NB) * DN_XG + g) * DN_NB + b % DN_NB, i)


def _dnscan(qkv, gb):
    cblk = MX // (DN_NB * DN_TT)
    rows = DN_NB * DN_TT

    def idx0(bp, s):
        return jnp.where(s == 0, cblk + bp, bp * DN_XG + s - 1)

    def idx1(bp, s):
        return jnp.where(s == 0, cblk + bp, bp * DN_XG + DN_XG - s)

    n_state = DN_NB * 2 * (H_DN // 2)
    est = 2 * 2 * (rows * 1536 * 2 + rows * 128 * 4 + rows * 512 * 4) + n_state * 4 * DK * DK * 4 + (4 << 20)
    return pl.pallas_call(
        _dnscan_kernel,
        grid=(B // DN_NB, 1 + DN_XG),
        in_specs=[
            pl.BlockSpec((rows, 3 * DN_W), lambda bp, s: (idx0(bp, s), 0)),
            pl.BlockSpec((rows, 3 * DN_W), lambda bp, s: (idx1(bp, s), 0)),
            pl.BlockSpec((rows, 128), lambda bp, s: (idx0(bp, s), 0)),
            pl.BlockSpec((rows, 128), lambda bp, s: (idx1(bp, s), 0)),
        ],
        out_specs=[
            pl.BlockSpec((rows, DN_W), lambda bp, s: (idx0(bp, s), 0)),
            pl.BlockSpec((rows, DN_W), lambda bp, s: (idx1(bp, s), 0)),
        ],
        out_shape=[jax.ShapeDtypeStruct((M, DN_W), F32), jax.ShapeDtypeStruct((M, DN_W), F32)],
        scratch_shapes=[pltpu.VMEM((n_state, 2 * DK, 2 * DK), F32)],
        compiler_params=_cparams(("arbitrary", "arbitrary"), est),
        name="dn_scan",
    )(qkv, qkv, gb, gb)


def _mlaproj_kernel(cq_ref, ckv_ref, sm_ref, cos_ref, sin_ref, gq_ref, gkv_ref,
                    wq1_ref, wq2_ref, wk_ref, wv_ref, q_ref, k_ref, v_ref):
    cq = cq_ref[...].astype(F32)
    hq = (cq * lax.rsqrt(jnp.sum(cq * cq, -1, keepdims=True) * (1.0 / Q_RANK) + EPS) * gq_ref[...]).astype(BF16)
    ckv = ckv_ref[...].astype(F32)
    hkv = (ckv * lax.rsqrt(jnp.mean(ckv * ckv, -1, keepdims=True) + EPS) * gkv_ref[...]).astype(BF16)
    q1 = _bdot(hq, wq1_ref[...])
    q2 = _bdot(hq, wq2_ref[...])
    kn = _bdot(hkv, wk_ref[...])
    vv = _bdot(hkv, wv_ref[...])
    cosq = cos_ref[...]
    sin = sin_ref[...]
    lane = lax.broadcasted_iota(jnp.int32, cosq.shape, 1)
    cosk = jnp.where(lane >= NOPE, cosq, 0.0)
    kpe = sm_ref[:, 0:HP] * cosk + sm_ref[:, HP:2 * HP] * sin
    for h in range(H_MLA):
        sl = slice(h * HP, (h + 1) * HP)
        q_ref[:, sl] = ((q1[:, sl] * cosq + q2[:, sl] * sin) * QK_SCALE).astype(BF16)
        k_ref[:, sl] = (kn[:, sl] + kpe).astype(BF16)
        v_ref[:, sl] = jnp.where(lane == V_DIM, 1.0, vv[:, sl]).astype(BF16)


def _mlaproj(u, usm, cos_t, sin_t, gq, gkv, wq1, wq2, wk, wv):
    tm = 256
    wide = H_MLA * HP
    est = 2 * (tm * (CQ_PAD + KV_RANK) * 2 + tm * 256 * 4 + 2 * tm * 128 * 4 + 2 * CQ_PAD * wide * 2
               + 2 * KV_RANK * wide * 2 + 3 * tm * wide * 2) + 6 * tm * wide * 4
    full = lambda shape: pl.BlockSpec(shape, lambda i: (0, 0))
    return pl.pallas_call(
        _mlaproj_kernel,
        grid=(M // tm,),
        in_specs=[
            pl.BlockSpec((tm, CQ_PAD), lambda i: (i, C_CQ // CQ_PAD)),
            pl.BlockSpec((tm, KV_RANK), lambda i: (i, C_CKV // KV_RANK)),
            pl.BlockSpec((tm, N_SMALL), lambda i: (i, 0)),
            pl.BlockSpec((tm, HP), lambda i: (i, 0)),
            pl.BlockSpec((tm, HP), lambda i: (i, 0)),
            full((1, CQ_PAD)), full((1, KV_RANK)),
            full((CQ_PAD, wide)), full((CQ_PAD, wide)), full((KV_RANK, wide)), full((KV_RANK, wide)),
        ],
        out_specs=[pl.BlockSpec((tm, wide), lambda i: (i, 0))] * 3,
        out_shape=[jax.ShapeDtypeStruct((M, wide), BF16)] * 3,
        compiler_params=_cparams(("arbitrary",), est),
        name="mla_proj",
    )(u, u, usm, cos_t, sin_t, gq, gkv, wq1, wq2, wk, wv)


ATT_TQ = 256
ATT_TK = 512
ATT_HG = 2


def _attn_kernel(q_ref, kx_ref, vx_ref, kc_ref, vc_ref, o_ref):
    qi = pl.program_id(2)
    nt = (((1,), (1,)), ((), ()))
    heads = [slice(g * HP, (g + 1) * HP) for g in range(ATT_HG)]
    qs = [q_ref[:, sl] for sl in heads]

    def scores(g, k_ref, rows):
        return lax.dot_general(qs[g], k_ref[rows, heads[g]], nt, preferred_element_type=F32)

    def update(carry, s, v_blk):
        m, acc = carry
        m_new = jnp.maximum(m, jnp.max(s, -1, keepdims=True))
        p = jnp.exp(s - m_new).astype(BF16)
        return m_new, jnp.exp(m - m_new) * acc + _bdot(p, v_blk)

    def finish(carries):
        for g, (_, acc) in enumerate(carries):
            o_ref[:, heads[g]] = (acc / acc[:, V_DIM:V_DIM + 1]).astype(o_ref.dtype)

    init = (jnp.full((ATT_TQ, 1), -jnp.inf, F32), jnp.zeros((ATT_TQ, HP), F32))
    everything = slice(None)

    def ctx_block():
        return [update(init, scores(g, kc_ref, everything), vc_ref[:, heads[g]]) for g in range(ATT_HG)]

    @pl.when(qi < T // ATT_TQ)
    def _():
        nblk = T // ATT_TK
        s_next = [scores(g, kx_ref, slice(0, ATT_TK)) for g in range(ATT_HG)]
        carries = ctx_block()
        for j in range(nblk):
            s = s_next
            if j + 1 < nblk:
                s_next = [scores(g, kx_ref, slice((j + 1) * ATT_TK, (j + 2) * ATT_TK)) for g in range(ATT_HG)]
            rows = slice(j * ATT_TK, (j + 1) * ATT_TK)
            carries = [update(carries[g], s[g], vx_ref[rows, heads[g]]) for g in range(ATT_HG)]
        finish(carries)

    @pl.when(qi >= T // ATT_TQ)
    def _():
        finish(ctx_block())


def _attention(q, k, v, with_ctx_queries):
    tq = ATT_TQ
    nqx = T // tq
    nq = nqx + (1 if with_ctx_queries else 0)
    cblk = MX // CTX
    rows = M if with_ctx_queries else MX

    def qidx(b, h, i):
        return (jnp.where(i < nqx, b * nqx + i, MX // tq + b), h)

    wide = ATT_HG * HP
    est = 2 * (2 * T * wide * 2 + 2 * CTX * wide * 2 + 2 * tq * wide * 2) + 8 * ATT_HG * tq * ATT_TK * 4
    return pl.pallas_call(
        _attn_kernel,
        grid=(B, H_MLA // ATT_HG, nq),
        in_specs=[
            pl.BlockSpec((tq, wide), qidx),
            pl.BlockSpec((T, wide), lambda b, h, i: (b, h)),
            pl.BlockSpec((T, wide), lambda b, h, i: (b, h)),
            pl.BlockSpec((CTX, wide), lambda b, h, i: (cblk + b, h)),
            pl.BlockSpec((CTX, wide), lambda b, h, i: (cblk + b, h)),
        ],
        out_specs=pl.BlockSpec((tq, wide), qidx),
        out_shape=jax.ShapeDtypeStruct((rows, H_MLA * HP), BF16),
        compiler_params=_cparams(("arbitrary", "arbitrary", "arbitrary"), est),
        name="mla_attention",
    )(q, k, v, k, v)


def _mm1_kernel(a_ref, b_ref, o_ref):
    o_ref[...] = _bdot(a_ref[...], b_ref[...]).astype(o_ref.dtype)


def _fourier_channels(u, w_ab):
    tm = 512
    est = 2 * (tm * F_W * 2 + F_W * 2 * F_W * 2 + tm * 2 * F_W * 2)
    return pl.pallas_call(
        _mm1_kernel,
        grid=(M // tm,),
        in_specs=[pl.BlockSpec((tm, F_W), lambda i: (i, C_F // F_W)),
                  pl.BlockSpec((F_W, 2 * F_W), lambda i: (0, 0))],
        out_specs=pl.BlockSpec((tm, 2 * F_W), lambda i: (i, 0)),
        out_shape=jax.ShapeDtypeStruct((M, 2 * F_W), BF16),
        compiler_params=_cparams(("arbitrary",), est),
        name="fourier_channels",
    )(u, w_ab)


def _dft_kernel(cs_ref, ab_ref, o_ref, acc_ref, *, nk):
    kk = pl.program_id(2)

    @pl.when(kk == 0)
    def _():
        acc_ref[...] = jnp.zeros_like(acc_ref)

    acc_ref[...] += _bdot(cs_ref[...], ab_ref[...])

    @pl.when(kk == nk - 1)
    def _():
        o_ref[...] = acc_ref[...].astype(o_ref.dtype)


def _fourier_tokens(cs, ab, seq, row0, tm, tk):
    nk = 2 * seq // tk
    kpb = seq // tk
    est = 2 * (tm * tk * 2 + tk * F_W * 2 + tm * F_W * 2) + tm * F_W * 4
    return pl.pallas_call(
        functools.partial(_dft_kernel, nk=nk),
        grid=(B, seq // tm, nk),
        in_specs=[
            pl.BlockSpec((tm, tk), lambda b, i, kk: (i, kk)),
            pl.BlockSpec((tk, F_W), lambda b, i, kk: (row0 // tk + b * kpb + kk % kpb, kk // kpb)),
        ],
        out_specs=pl.BlockSpec((tm, F_W), lambda b, i, kk: (b * (seq // tm) + i, 0)),
        out_shape=jax.ShapeDtypeStruct((B * seq, F_W), BF16),
        scratch_shapes=[pltpu.VMEM((tm, F_W), F32)],
        compiler_params=_cparams(("arbitrary", "arbitrary", "arbitrary"), est),
        name="fourier_tokens",
    )(cs, ab)


def _merge_kernel(x_ref, mod_ref, o0_ref, o1_ref, z_ref, yb_ref, ycx_ref, ycc_ref, gt_ref, gdn_ref,
                  wa_ref, wb_ref, wc_ref, wo_ref, out_ref, *, x_tiles):
    yc = jnp.where(pl.program_id(0) < x_tiles, ycx_ref[...], ycc_ref[...])
    o = o0_ref[...] + o1_ref[...]
    z = z_ref[...].astype(F32)
    gdn = gdn_ref[...]
    ya = []
    for h in range(H_DN):
        sl = slice(h * DK, (h + 1) * DK)
        oh = o[:, sl]
        zh = z[:, sl]
        yh = oh * lax.rsqrt(jnp.mean(oh * oh, -1, keepdims=True) + EPS) * gdn
        ya.append((yh * (zh * jax.nn.sigmoid(zh))).astype(BF16))
    ya = jnp.concatenate(ya, -1)
    s = jax.nn.sigmoid(gt_ref[:, 0:D].astype(F32)) * _bdot(ya, wa_ref[...])
    s = s + jax.nn.sigmoid(gt_ref[:, D:2 * D].astype(F32)) * _bdot(yb_ref[...], wb_ref[...])
    s = s + jax.nn.sigmoid(gt_ref[:, 2 * D:3 * D].astype(F32)) * _bdot(yc, wc_ref[...])
    mix = _bdot(s.astype(BF16), wo_ref[...])
    out_ref[...] = x_ref[...] + mod_ref[2:3, :] * mix


def _merge(X, mod, o0, o1, u, yb, ycx, ycc, gdn, wa, wb, wc, wo, rows):
    tm = 256
    x_tiles = MX // tm
    full = lambda shape: pl.BlockSpec(shape, lambda i: (0, 0))
    est = 2 * (2 * tm * D * 4 + 2 * tm * DN_W * 4 + tm * DN_W * 2 * 3 + tm * D * 2 + tm * 3 * D * 2
               + (2 * DN_W + 2 * D) * D * 2) + 6 * tm * D * 4
    return pl.pallas_call(
        functools.partial(_merge_kernel, x_tiles=x_tiles),
        grid=(rows // tm,),
        in_specs=[
            pl.BlockSpec((tm, D), lambda i: (i, 0)),
            _mod_spec(tm),
            pl.BlockSpec((tm, DN_W), lambda i: (_dn_group(i), 0)),
            pl.BlockSpec((tm, DN_W), lambda i: (_dn_group(i), 0)),
            pl.BlockSpec((tm, DN_W), lambda i: (i, C_Z // DN_W)),
            pl.BlockSpec((tm, H_MLA * HP), lambda i: (i, 0)),
            pl.BlockSpec((tm, F_W), lambda i: (jnp.minimum(i, x_tiles - 1), 0)),
            pl.BlockSpec((tm, F_W), lambda i: (jnp.maximum(i - x_tiles, 0), 0)),
            pl.BlockSpec((tm, 3 * D), lambda i: (i, 0)),
            full((1, DK)),
            full((DN_W, D)), full((H_MLA * HP, D)), full((F_W, D)), full((D, D)),
        ],
        out_specs=pl.BlockSpec((tm, D), lambda i: (i, 0)),
        out_shape=jax.ShapeDtypeStruct((rows, D), F32),
        compiler_params=_cparams(("arbitrary",), est),
        name="merge",
    )(X, mod, o0, o1, u, yb, ycx, ycc, u, gdn, wa, wb, wc, wo)


def _ffn_kernel(x_ref, g_ref, mod_ref, wg_ref, wu_ref, wd_ref, o_ref, h_ref, acc_ref, *, nf):
    f = pl.program_id(1)

    @pl.when(f == 0)
    def _():
        h_ref[...] = _norm_mod(x_ref[...], g_ref[...], mod_ref[3:4, :], mod_ref[4:5, :]).astype(BF16)
        acc_ref[...] = jnp.zeros_like(acc_ref)

    h = h_ref[...]
    a = _bdot(h, wg_ref[...])
    up = _bdot(h, wu_ref[...])
    act = (a * jax.nn.sigmoid(a) * up).astype(BF16)
    acc_ref[...] += _bdot(act, wd_ref[...])

    @pl.when(f == nf - 1)
    def _():
        o_ref[...] = x_ref[...] + mod_ref[5:6, :] * acc_ref[...]


def _ffn(X, gain, mod, wg, wu, wd):
    tm, tf = 512, 1408
    nf = D_FF // tf
    est = 2 * (2 * tm * D * 4 + 3 * D * tf * 2) + tm * D * 6 + 3 * tm * tf * 4
    return pl.pallas_call(
        functools.partial(_ffn_kernel, nf=nf),
        grid=(M // tm, nf),
        in_specs=[
            pl.BlockSpec((tm, D), lambda i, f: (i, 0)),
            pl.BlockSpec((1, D), lambda i, f: (0, 0)),
            _mod_spec(tm),
            pl.BlockSpec((D, tf), lambda i, f: (0, f)),
            pl.BlockSpec((D, tf), lambda i, f: (0, f)),
            pl.BlockSpec((tf, D), lambda i, f: (f, 0)),
        ],
        out_specs=pl.BlockSpec((tm, D), lambda i, f: (i, 0)),
        out_shape=jax.ShapeDtypeStruct((M, D), F32),
        scratch_shapes=[pltpu.VMEM((tm, D), BF16), pltpu.VMEM((tm, D), F32)],
        compiler_params=_cparams(("arbitrary", "arbitrary"), est),
        name="ffn_dense",
    )(X, gain, mod, wg, wu, wd)


def _router_kernel(x_ref, g_ref, mod_ref, wr_ref, h_ref, r_ref):
    h = _norm_mod(x_ref[...], g_ref[...], mod_ref[3:4, :], mod_ref[4:5, :])
    h_ref[...] = h.astype(BF16)
    h_hi = h.astype(BF16)
    h_lo = (h - h_hi.astype(F32)).astype(BF16)
    wr = wr_ref[...]
    w_hi = wr.astype(BF16)
    w_lo = (wr - w_hi.astype(F32)).astype(BF16)
    logits = _bdot(h_hi, w_hi) + _bdot(h_lo, w_hi) + _bdot(h_hi, w_lo)
    lane = lax.broadcasted_iota(jnp.int32, logits.shape, 1)
    logits = jnp.where(lane < N_EXP, logits, -jnp.inf)
    m1 = jnp.max(logits, -1, keepdims=True)
    i1 = jnp.min(jnp.where(logits == m1, lane, 128), -1, keepdims=True)
    rest = jnp.where(lane == i1, -jnp.inf, logits)
    m2 = jnp.max(rest, -1, keepdims=True)
    i2 = jnp.min(jnp.where(rest == m2, lane, 128), -1, keepdims=True)
    e2 = jnp.exp(m2 - m1)
    w1 = 1.0 / (1.0 + e2)
    w2 = e2 / (1.0 + e2)
    r = jnp.where(lane == 0, i1.astype(F32), 0.0)
    r = jnp.where(lane == 1, i2.astype(F32), r)
    r = jnp.where(lane == 2, w1, r)
    r = jnp.where(lane == 3, w2, r)
    r_ref[...] = r


def _router(X, gain, mod, wr_pad):
    tm = 512
    est = 2 * (tm * D * 4 + D * 128 * 4 + tm * D * 2 + tm * 128 * 4) + 4 * tm * D * 4
    return pl.pallas_call(
        _router_kernel,
        grid=(MX // tm,),
        in_specs=[
            pl.BlockSpec((tm, D), lambda i: (i, 0)),
            pl.BlockSpec((1, D), lambda i: (0, 0)),
            _mod_spec(tm),
            pl.BlockSpec((D, 128), lambda i: (0, 0)),
        ],
        out_specs=[pl.BlockSpec((tm, D), lambda i: (i, 0)), pl.BlockSpec((tm, 128), lambda i: (i, 0))],
        out_shape=[jax.ShapeDtypeStruct((MX, D), BF16), jax.ShapeDtypeStruct((MX, 128), F32)],
        compiler_params=_cparams(("arbitrary",), est),
        name="moe_router",
    )(X, gain, mod, wr_pad)


MOE_TM = 512
MOE_TF = 1792
MOE_ROWS = MX * TOP_K + N_EXP * MOE_TM
MOE_TILES = MOE_ROWS // MOE_TM


def _moe_kernel(te_ref, nt_ref, h_ref, rw_ref, wg_ref, wu_ref, wd_ref, o_ref, acc_ref, *, nf):
    t = pl.program_id(0)
    f = pl.program_id(1)
    valid = t < nt_ref[0]

    @pl.when(jnp.logical_and(valid, f == 0))
    def _():
        acc_ref[...] = jnp.zeros_like(acc_ref)

    @pl.when(valid)
    def _():
        h = h_ref[...]
        a = _bdot(h, wg_ref[...])
        up = _bdot(h, wu_ref[...])
        act = (a * jax.nn.sigmoid(a) * up).astype(BF16)
        acc_ref[...] += _bdot(act, wd_ref[...])

    @pl.when(jnp.logical_and(valid, f == nf - 1))
    def _():
        o_ref[...] = (acc_ref[...] * rw_ref[...]).astype(o_ref.dtype)

    @pl.when(jnp.logical_and(jnp.logical_not(valid), f == nf - 1))
    def _():
        o_ref[...] = jnp.zeros_like(o_ref)


def _moe_experts(tile_expert, n_tiles, h_sorted, row_w, wg, wu, wd):
    tm, tf = MOE_TM, MOE_TF
    nf = E_FF // tf

    def fsel(t, f, nt):
        return jnp.where(t < nt[0], f, nf - 1)

    est = 2 * (tm * D * 2 + tm * 128 * 4 + 3 * D * tf * 2 + tm * D * 2) + tm * D * 4 + 3 * tm * tf * 4
    grid_spec = pltpu.PrefetchScalarGridSpec(
        num_scalar_prefetch=2,
        grid=(MOE_TILES, nf),
        in_specs=[
            pl.BlockSpec((tm, D), lambda t, f, te, nt: (t, 0)),
            pl.BlockSpec((tm, 1), lambda t, f, te, nt: (t, 0)),
            pl.BlockSpec((None, D, tf), lambda t, f, te, nt: (te[t], 0, fsel(t, f, nt))),
            pl.BlockSpec((None, D, tf), lambda t, f, te, nt: (te[t], 0, fsel(t, f, nt))),
            pl.BlockSpec((None, tf, D), lambda t, f, te, nt: (te[t], fsel(t, f, nt), 0)),
        ],
        out_specs=pl.BlockSpec((tm, D), lambda t, f, te, nt: (t, 0)),
        scratch_shapes=[pltpu.VMEM((tm, D), F32)],
    )
    return pl.pallas_call(
        functools.partial(_moe_kernel, nf=nf),
        grid_spec=grid_spec,
        out_shape=jax.ShapeDtypeStruct((MOE_ROWS, D), BF16),
        compiler_params=_cparams(("arbitrary", "arbitrary"), est),
        name="moe_experts",
    )(tile_expert, n_tiles, h_sorted, row_w, wg, wu, wd)


def _final_kernel(x_ref, mod_ref, y0_ref, y1_ref, g_ref, o_ref):
    x = x_ref[...] + mod_ref[5:6, :] * (y0_ref[...].astype(F32) + y1_ref[...].astype(F32))
    o_ref[...] = x * lax.rsqrt(jnp.mean(x * x, -1, keepdims=True) + EPS) * g_ref[...]


def _final(X, mod, y0, y1, gain):
    tm = 512
    est = 2 * (2 * tm * D * 4 + 2 * tm * D * 2) + 3 * tm * D * 4
    return pl.pallas_call(
        _final_kernel,
        grid=(MX // tm,),
        in_specs=[
            pl.BlockSpec((tm, D), lambda i: (i, 0)),
            _mod_spec(tm),
            pl.BlockSpec((tm, D), lambda i: (i, 0)),
            pl.BlockSpec((tm, D), lambda i: (i, 0)),
            pl.BlockSpec((1, D), lambda i: (0, 0)),
        ],
        out_specs=pl.BlockSpec((tm, D), lambda i: (i, 0)),
        out_shape=jax.ShapeDtypeStruct((MX, D), F32),
        compiler_params=_cparams(("arbitrary",), est),
        name="final_norm",
    )(X, mod, y0, y1, gain)


def _moe_routing(route):
    tm = MOE_TM
    e_flat = route[:, 0:TOP_K].astype(jnp.int32).reshape(-1)
    w_flat = route[:, TOP_K:2 * TOP_K].reshape(-1)
    n = e_flat.shape[0]
    eids = jnp.arange(N_EXP, dtype=jnp.int32)[None, :]

    def pick(onehot, table):
        return jnp.sum(onehot * table[None, :], 1)

    order = jnp.argsort(e_flat, stable=True).astype(jnp.int32)
    inv_order = jnp.argsort(order).astype(jnp.int32)
    oh_f = (e_flat[:, None] == eids).astype(jnp.int32)
    counts = jnp.sum(oh_f, 0)
    padded = ((counts + tm - 1) // tm) * tm
    start = jnp.cumsum(counts) - counts
    pend = jnp.cumsum(padded)
    pstart = pend - padded
    slot_row = pick(oh_f, pstart - start) + inv_order

    n_tiles = (pend[-1] // tm).astype(jnp.int32)
    tile_start = jnp.arange(MOE_TILES, dtype=jnp.int32) * tm
    tile_expert = jnp.minimum(jnp.sum(tile_start[:, None] >= pend[None, :], 1), N_EXP - 1).astype(jnp.int32)
    rows = jnp.arange(MOE_ROWS, dtype=jnp.int32)
    oh_r = (jnp.repeat(tile_expert, tm)[:, None] == eids).astype(jnp.int32)
    rank_r = rows - pick(oh_r, pstart)
    valid = jnp.logical_and(rank_r < pick(oh_r, counts), rows < pend[-1])
    slot_r = order[jnp.clip(pick(oh_r, start) + rank_r, 0, n - 1)]
    row_token = jnp.where(valid, slot_r // TOP_K, 0)
    row_w = jnp.where(valid, w_flat[slot_r], 0.0)
    last_expert = jnp.sum(jnp.where(jnp.arange(MOE_TILES) == n_tiles - 1, tile_expert, 0))
    tile_expert = jnp.where(tile_start < pend[-1], tile_expert, last_expert)
    return row_token, row_w.reshape(MOE_ROWS, 1), slot_row.reshape(MX, TOP_K), tile_expert, n_tiles.reshape(1)


def _dft_table(n):
    split = 64 if n > 64 else 1
    t = jnp.arange(n, dtype=jnp.int32)[None, :]
    a = jnp.arange(n // split, dtype=jnp.int32)[:, None]
    b = jnp.arange(split, dtype=jnp.int32)[:, None]
    na = n // split
    ang_a = ((a * t) % na).astype(F32) * (2.0 * np.pi / na)
    ang_b = ((b * t) % n).astype(F32) * (2.0 * np.pi / n)
    scale = n ** -0.5
    ca, sa = jnp.cos(ang_a)[:, None, :], jnp.sin(ang_a)[:, None, :]
    cb, sb = (jnp.cos(ang_b) * scale)[None, :, :], (jnp.sin(ang_b) * scale)[None, :, :]
    c = (ca * cb - sa * sb).reshape(n, n)
    s = (sa * cb + ca * sb).reshape(n, n)
    return jnp.concatenate([c, s], 1).astype(BF16)


def _channel_table():
    k = jnp.arange(F_GDIM, dtype=jnp.int32)
    ang = ((k[:, None] * k[None, :]) % F_GDIM).astype(F32) * (2.0 * np.pi / F_GDIM)
    scale = F_GDIM ** -0.5
    eye = jnp.eye(F_GROUPS, dtype=F32)
    return jnp.concatenate([jnp.kron(eye, jnp.cos(ang) * scale), jnp.kron(eye, -jnp.sin(ang) * scale)], 1).astype(BF16)


def _rope_tables():
    t = jnp.arange(T, dtype=jnp.int32)
    r = (t // GRID_W).astype(F32)[:, None]
    c = (t % GRID_W).astype(F32)[:, None]
    axis_dim = ROPE // 2
    inv = ROPE_BASE ** (-jnp.arange(0, axis_dim, 2, dtype=F32) / axis_dim)[None, :]
    cr, sr, cc, sc = jnp.cos(r * inv), jnp.sin(r * inv), jnp.cos(c * inv), jnp.sin(c * inv)
    cos32 = jnp.concatenate([cr, cr, cc, cc], 1)
    sin32 = jnp.concatenate([-sr, sr, -sc, sc], 1)
    ones = jnp.ones((T, NOPE), F32)
    zeros = jnp.zeros((T, NOPE), F32)
    tail = jnp.zeros((T, HP - NOPE - ROPE), F32)
    cos_x = jnp.concatenate([ones, cos32, tail], 1)
    sin_x = jnp.concatenate([zeros, sin32, tail], 1)
    cos_c = jnp.concatenate([jnp.ones((MC, NOPE + ROPE), F32), jnp.zeros((MC, HP - NOPE - ROPE), F32)], 1)
    cos_t = jnp.concatenate([jnp.tile(cos_x, (B, 1)), cos_c], 0)
    sin_t = jnp.concatenate([jnp.tile(sin_x, (B, 1)), jnp.zeros((MC, HP), F32)], 0)
    return cos_t, sin_t


_ROPE_SWAP = np.concatenate([np.arange(8, 16), np.arange(0, 8), np.arange(24, 32), np.arange(16, 24)])


def _layer_weights(w_in, w_qup, w_kvup, w_branch):
    sizes = (DN_W, DN_W, DN_W, DN_W, 2 * H_DN, 2 * H_DN, Q_RANK, KV_RANK, ROPE, F_W, 3 * D)
    offs = np.concatenate([[0], np.cumsum(sizes)])
    seg = [w_in[:, offs[i]:offs[i + 1]] for i in range(len(sizes))]
    dq, dk, dv, dz, da, db, cq, ckv, kpe, fo, gates = seg
    zc = lambda n: jnp.zeros((D, n), F32)
    w_main = jnp.concatenate([gates, dq, dk, dv, dz, fo, cq, zc(CQ_PAD - Q_RANK), ckv], 1).astype(BF16)
    w_small = jnp.concatenate([da, db, zc(NOPE - 4 * H_DN), kpe, zc(HP - NOPE - ROPE),
                               zc(NOPE), kpe[:, _ROPE_SWAP], zc(HP - NOPE - ROPE)], 1).astype(BF16)

    qh = w_qup.reshape(Q_RANK, H_MLA, NOPE + ROPE)
    zq = lambda n: jnp.zeros((Q_RANK, H_MLA, n), F32)
    wq1 = jnp.concatenate([qh, zq(HP - NOPE - ROPE)], 2).reshape(Q_RANK, H_MLA * HP)
    wq2 = jnp.concatenate([zq(NOPE), qh[:, :, NOPE:][:, :, _ROPE_SWAP], zq(HP - NOPE - ROPE)], 2).reshape(Q_RANK, H_MLA * HP)
    rowpad = jnp.zeros((CQ_PAD - Q_RANK, H_MLA * HP), F32)
    wq1 = jnp.concatenate([wq1, rowpad], 0).astype(BF16)
    wq2 = jnp.concatenate([wq2, rowpad], 0).astype(BF16)
    kvh = w_kvup.reshape(KV_RANK, H_MLA, NOPE + V_DIM)
    zk = jnp.zeros((KV_RANK, H_MLA, HP - NOPE), F32)
    wk = jnp.concatenate([kvh[:, :, :NOPE], zk], 2).reshape(KV_RANK, H_MLA * HP).astype(BF16)
    wv = jnp.concatenate([kvh[:, :, NOPE:], zk], 2).reshape(KV_RANK, H_MLA * HP).astype(BF16)

    wb = w_branch[1].reshape(H_MLA, V_DIM, D)
    wb = jnp.concatenate([wb, jnp.zeros((H_MLA, HP - V_DIM, D), F32)], 1).reshape(H_MLA * HP, D)
    return dict(w_main=w_main, w_small=w_small, wq1=wq1, wq2=wq2, wk=wk, wv=wv,
                wa=w_branch[0].astype(BF16), wb=wb.astype(BF16), wc=w_branch[2].astype(BF16))


def kernel(x, c, ctx, c_ctx, w_mod, b_mod, norm_mix, norm_ffn, w_in, dn_conv, dn_a_log, dn_dt_bias, dn_norm, mla_q_norm, mla_kv_norm, mla_w_qup, mla_w_kvup, w_branch, w_out, ffn_w_gate, ffn_w_up, ffn_w_down, moe_router, moe_w_gate, moe_w_up, moe_w_down, final_norm):
    X = jnp.concatenate([x.reshape(MX, D), ctx.reshape(MC, D)], 0)
    mods = _mod_vectors(c, c_ctx, w_mod, b_mod)
    cos_t, sin_t = _rope_tables()
    cs_x = _dft_table(T)
    cs_c = _dft_table(CTX)
    w_ab = _channel_table()

    out = None
    for l in range(DEPTH):
        last = l == DEPTH - 1
        mod = mods[l]
        lw = _layer_weights(w_in[l], mla_w_qup[l], mla_w_kvup[l], w_branch[l])
        u, usm = _inproj(X, norm_mix[l].reshape(1, D), mod, lw["w_main"], lw["w_small"])
        qkv, gb = _dnpre(u, usm, dn_conv[l], dn_a_log[l], dn_dt_bias[l])
        o0, o1 = _dnscan(qkv, gb)
        gq = jnp.concatenate([mla_q_norm[l], jnp.zeros((CQ_PAD - Q_RANK,), F32)]).reshape(1, CQ_PAD)
        q, k, v = _mlaproj(u, usm, cos_t, sin_t, gq, mla_kv_norm[l].reshape(1, KV_RANK),
                           lw["wq1"], lw["wq2"], lw["wk"], lw["wv"])
        yb = _attention(q, k, v, with_ctx_queries=not last)
        ab = _fourier_channels(u, w_ab)
        ycx = _fourier_tokens(cs_x, ab, T, 0, 1024, 2048)
        ycc = ycx if last else _fourier_tokens(cs_c, ab, CTX, MX, CTX, CTX)
        rows = MX if last else M
        X = _merge(X, mod, o0, o1, u, yb, ycx, ycc, dn_norm[l].reshape(1, DK),
                   lw["wa"], lw["wb"], lw["wc"], w_out[l].astype(BF16), rows)
        e = l // 2
        if l % 2 == 0:
            X = _ffn(X, norm_ffn[l].reshape(1, D), mod, ffn_w_gate[e].astype(BF16),
                     ffn_w_up[e].astype(BF16), ffn_w_down[e].astype(BF16))
        else:
            wr = jnp.concatenate([moe_router[e], jnp.zeros((D, 128 - N_EXP), F32)], 1)
            h, route = _router(X, norm_ffn[l].reshape(1, D), mod, wr)
            row_token, row_w, slot_row, tile_expert, n_tiles = _moe_routing(route)
            h_sorted = jnp.take(h, lax.optimization_barrier(row_token), axis=0)
            y = _moe_experts(tile_expert, n_tiles, h_sorted, row_w, moe_w_gate[e].astype(BF16),
                             moe_w_up[e].astype(BF16), moe_w_down[e].astype(BF16))
            y0 = jnp.take(y, slot_row[:, 0], axis=0)
            y1 = jnp.take(y, slot_row[:, 1], axis=0)
            out = _final(X, mod, y0, y1, final_norm.reshape(1, D))
    return out.reshape(B, T, D)
```

```python
import functools

import numpy as np
import jax
import jax.numpy as jnp
from jax import lax
from jax.experimental import pallas as pl
from jax.experimental.pallas import tpu as pltpu

F32 = jnp.float32
BF16 = jnp.bfloat16

D = 1024
B = 4
T = 4096
CTX = 256
DEPTH = 2
GRID_W = 64
H_DN = 4
DK = 128
DN_W = H_DN * DK
CONV_K = 5
CHUNK = 64
H_MLA = 8
Q_RANK = 384
KV_RANK = 256
NOPE = 64
ROPE = 32
V_DIM = 64
QK_SCALE = (NOPE + ROPE) ** -0.5
ROPE_BASE = 10000.0
F_GROUPS = 4
F_GDIM = 128
F_W = F_GROUPS * F_GDIM
D_FF = 2816
N_EXP = 8
TOP_K = 2
E_FF = 3584
EPS = 1e-6

MX = B * T
MC = B * CTX
M = MX + MC

HP = 128
C_GATES = 0
C_QKV = 3 * D
C_Z = C_QKV + 3 * DN_W
C_F = C_Z + DN_W
C_CQ = C_F + F_W
CQ_PAD = 512
C_CKV = C_CQ + CQ_PAD
N_MAIN = C_CKV + KV_RANK
N_SMALL = 256

V7X_VMEM_BYTES = 64 * 1024 * 1024
V7X_VMEM_USABLE = 56 * 1024 * 1024


def _cparams(sem, vmem_est_bytes):
    limit = int(min(V7X_VMEM_USABLE, max(32 * 1024 * 1024, 2 * vmem_est_bytes)))
    return pltpu.CompilerParams(dimension_semantics=sem, vmem_limit_bytes=limit)


def _norm_mod(x, gain, shift, scale):
    y = x * lax.rsqrt(jnp.mean(x * x, -1, keepdims=True) + EPS) * gain
    return y * (1.0 + scale) + shift


def _bdot(a, b):
    return jnp.dot(a, b, preferred_element_type=F32)


def _split3(v):
    hi = v.astype(BF16)
    r1 = v - hi.astype(F32)
    mid = r1.astype(BF16)
    lo = (r1 - mid.astype(F32)).astype(BF16)
    return hi, mid, lo


def _mod_kernel(a_ref, w_ref, b_ref, o_ref):
    a = a_ref[...]
    a = a * jax.nn.sigmoid(a)
    w = w_ref[...]
    a_hi = a.astype(BF16)
    a_lo = (a - a_hi.astype(F32)).astype(BF16)
    w_hi = w.astype(BF16)
    w_lo = (w - w_hi.astype(F32)).astype(BF16)
    o_ref[...] = _bdot(a_hi, w_hi) + _bdot(a_lo, w_hi) + _bdot(a_hi, w_lo) + b_ref[...]


def _mod_vectors(c, c_ctx, w_mod, b_mod):
    tn = 1536
    a = jnp.concatenate([c, c_ctx[None, :], jnp.zeros((3, D), F32)], 0)
    out = pl.pallas_call(
        _mod_kernel,
        grid=(DEPTH, 6 * D // tn),
        in_specs=[
            pl.BlockSpec((8, D), lambda l, j: (0, 0)),
            pl.BlockSpec((None, D, tn), lambda l, j: (l, 0, j)),
            pl.BlockSpec((None, 1, tn), lambda l, j: (l, 0, j)),
        ],
        out_specs=pl.BlockSpec((None, 8, tn), lambda l, j: (l, 0, j)),
        out_shape=jax.ShapeDtypeStruct((DEPTH, 8, 6 * D), F32),
        compiler_params=_cparams(("arbitrary", "arbitrary"), 2 * D * tn * 4 * 2),
        name="mod_vectors",
    )(a, w_mod, b_mod.reshape(DEPTH, 1, 6 * D))
    return out.reshape(DEPTH, 8, 6, D)[:, : B + 1]


def _mod_spec(tm):
    tpb = T // tm
    return pl.BlockSpec((None, 6, D), lambda i, *_: (jnp.minimum(i // tpb, B), 0, 0))


def _inproj_kernel(x_ref, g_ref, mod_ref, w_ref, ws_ref, o_ref, os_ref, h_ref):
    @pl.when(pl.program_id(1) == 0)
    def _():
        hb = _norm_mod(x_ref[...], g_ref[...], mod_ref[0:1, :], mod_ref[1:2, :]).astype(BF16)
        h_ref[...] = hb
        os_ref[...] = _bdot(hb, ws_ref[...])

    o_ref[...] = _bdot(h_ref[...], w_ref[...]).astype(o_ref.dtype)


def _inproj(X, gain, mod, w_main, w_small):
    tm, tn = 1024, 3200
    est = 2 * (tm * D * 4 + D * tn * 2 + D * N_SMALL * 2 + tm * tn * 2 + tm * N_SMALL * 4) + tm * D * 2
    return pl.pallas_call(
        _inproj_kernel,
        grid=(M // tm, N_MAIN // tn),
        in_specs=[
            pl.BlockSpec((tm, D), lambda i, j: (i, 0)),
            pl.BlockSpec((1, D), lambda i, j: (0, 0)),
            _mod_spec(tm),
            pl.BlockSpec((D, tn), lambda i, j: (0, j)),
            pl.BlockSpec((D, N_SMALL), lambda i, j: (0, 0)),
        ],
        out_specs=[
            pl.BlockSpec((tm, tn), lambda i, j: (i, j)),
            pl.BlockSpec((tm, N_SMALL), lambda i, j: (i, 0)),
        ],
        out_shape=[jax.ShapeDtypeStruct((M, N_MAIN), BF16), jax.ShapeDtypeStruct((M, N_SMALL), F32)],
        scratch_shapes=[pltpu.VMEM((tm, D), BF16)],
        compiler_params=_cparams(("arbitrary", "arbitrary"), est),
        name="in_proj",
    )(X, gain, mod, w_main, w_small)


DN_TT = 256
DN_HALO = 16


def _dnpre_kernel(cur_ref, prev_ref, next_ref, sm_ref, cw_ref, al_ref, dtb_ref, qkv_ref, gb_ref):
    i = pl.program_id(0)
    tiles_per_seq = T // DN_TT
    is_ctx = i >= MX // DN_TT
    first = jnp.logical_or(is_ctx, i % tiles_per_seq == 0)
    last = jnp.logical_or(is_ctx, i % tiles_per_seq == tiles_per_seq - 1)
    prev = jnp.where(first, 0.0, prev_ref[...].astype(F32))
    nxt = jnp.where(last, 0.0, next_ref[...].astype(F32))
    xc = jnp.concatenate([prev, cur_ref[...].astype(F32), nxt], axis=0)
    n = xc.shape[0]
    half = CONV_K // 2
    acc = xc * cw_ref[half:half + 1, :]
    for j in range(CONV_K):
        if j != half:
            acc = acc + pltpu.roll(xc, (half - j) % n, 0) * cw_ref[j:j + 1, :]
    y = acc[DN_HALO:DN_HALO + DN_TT]
    y = y * jax.nn.sigmoid(y)
    for p in range(3 * H_DN):
        blk = y[:, p * DK:(p + 1) * DK]
        if p < 2 * H_DN:
            blk = blk * lax.rsqrt(jnp.sum(blk * blk, -1, keepdims=True) + EPS)
        qkv_ref[:, p * DK:(p + 1) * DK] = blk.astype(BF16)

    sm = sm_ref[...]
    lane = lax.broadcasted_iota(jnp.int32, sm.shape, 1)
    g = -jnp.exp(al_ref[...]) * jax.nn.softplus(sm + dtb_ref[...])
    gb_ref[...] = jnp.where(lane < 2 * H_DN, g, jax.nn.sigmoid(sm))


def _dnpre(u, usm, conv_w, a_log, dt_bias):
    tt = DN_TT
    nb16 = tt // DN_HALO
    qkv_cb = C_QKV // (3 * DN_W)
    pad = jnp.zeros((1, 128 - 2 * H_DN), F32)
    al = jnp.concatenate([a_log.reshape(1, 2 * H_DN), pad], 1)
    dtb = jnp.concatenate([dt_bias.reshape(1, 2 * H_DN), pad], 1)
    est = 2 * (tt * 1536 * 2 * 2 + tt * 128 * 4 * 2) + 6 * (tt + 32) * 1536 * 4
    return pl.pallas_call(
        _dnpre_kernel,
        grid=(M // tt,),
        in_specs=[
            pl.BlockSpec((tt, 3 * DN_W), lambda i: (i, qkv_cb)),
            pl.BlockSpec((DN_HALO, 3 * DN_W), lambda i: (jnp.maximum(i * nb16 - 1, 0), qkv_cb)),
            pl.BlockSpec((DN_HALO, 3 * DN_W), lambda i: (jnp.minimum((i + 1) * nb16, M // DN_HALO - 1), qkv_cb)),
            pl.BlockSpec((tt, 128), lambda i: (i, 0)),
            pl.BlockSpec((CONV_K, 3 * DN_W), lambda i: (0, 0)),
            pl.BlockSpec((1, 128), lambda i: (0, 0)),
            pl.BlockSpec((1, 128), lambda i: (0, 0)),
        ],
        out_specs=[
            pl.BlockSpec((tt, 3 * DN_W), lambda i: (_dn_group(i), 0)),
            pl.BlockSpec((tt, 128), lambda i: (_dn_group(i), 0)),
        ],
        out_shape=[jax.ShapeDtypeStruct((M, 3 * DN_W), BF16), jax.ShapeDtypeStruct((M, 128), F32)],
        compiler_params=_cparams(("arbitrary",), est),
        name="dn_pre",
    )(u, u, u, usm, conv_w, al, dtb)


DN_GC = DN_TT // CHUNK
DN_XG = T // DN_TT
DN_NB = 4


def _dnscan_kernel(qkv0_ref, qkv1_ref, gb0_ref, gb1_ref, o0_ref, o1_ref, s_ref):
    @pl.when(pl.program_id(1) == 0)
    def _():
        s_ref[...] = jnp.zeros_like(s_ref)

    def body(j, carry):
        _dnscan_chunk(j, qkv0_ref, qkv1_ref, gb0_ref, gb1_ref, o0_ref, o1_ref, s_ref)
        return carry

    lax.fori_loop(0, DN_GC, body, 0)


def _dnscan_chunk(j, qkv0_ref, qkv1_ref, gb0_ref, gb1_ref, o0_ref, o1_ref, s_ref):
    C = CHUNK
    W2 = 2 * DK
    ri = lax.broadcasted_iota(jnp.int32, (C, 2 * C), 0)
    li = lax.broadcasted_iota(jnp.int32, (C, 2 * C), 1)
    lm = li % C
    first = li < C
    firstw = lax.broadcasted_iota(jnp.int32, (C, W2), 1) < DK
    eye2 = (ri == lm).astype(F32)
    ii = lax.broadcasted_iota(jnp.int32, (C, C), 0)
    jj = lax.broadcasted_iota(jnp.int32, (C, C), 1)
    sr = lax.broadcasted_iota(jnp.int32, (W2, W2), 0) < DK
    sc = lax.broadcasted_iota(jnp.int32, (W2, W2), 1) < DK
    sdiag = sr == sc
    qscale = DK ** -0.5
    nt = (((1,), (1,)), ((), ()))
    tn = (((0,), (0,)), ((), ()))

    def bd(m2):
        return jnp.concatenate([jnp.where(first, m2, 0.0), jnp.where(first, 0.0, m2)], 0).astype(BF16)

    def bdw(m2):
        return jnp.concatenate([jnp.where(firstw, m2, 0.0), jnp.where(firstw, 0.0, m2)], 0).astype(BF16)

    ch = []
    for bb, d in [(bb, d) for bb in range(DN_NB) for d in range(2)]:
        rows = pl.ds(pl.multiple_of(bb * DN_TT + (j if d == 0 else DN_GC - 1 - j) * C, C), C)
        qkv_ref = (qkv0_ref, qkv1_ref)[d]
        gbv = (gb0_ref, gb1_ref)[d][rows, :]
        incl = (ri >= lm) if d == 0 else (ri <= lm)
        strict = (ri > lm) if d == 0 else (ri < lm)
        tri = ((ii >= jj) if d == 0 else (ii <= jj)).astype(BF16)
        g_hi, g_mid, g_lo = _split3(gbv)
        gc = _bdot(tri, g_hi) + _bdot(tri, g_mid) + _bdot(tri, g_lo)
        gct = jnp.concatenate([gc, gc], 0).T
        last_row = C - 1 if d == 0 else 0
        for p in range(H_DN // 2):
            la = d * H_DN + 2 * p
            lb = la + 1
            col2 = jnp.where(first, gc[:, la:la + 1], gc[:, lb:lb + 1])
            row2 = jnp.where(first[0:1], gct[la:la + 1, :], gct[lb:lb + 1, :])
            colw = jnp.where(firstw, gc[:, la:la + 1], gc[:, lb:lb + 1])
            betaw = jnp.where(firstw, gbv[:, 2 * H_DN + la:2 * H_DN + la + 1], gbv[:, 2 * H_DN + lb:2 * H_DN + lb + 1])
            gl_a = gc[last_row:last_row + 1, la:la + 1]
            gl_b = gc[last_row:last_row + 1, lb:lb + 1]
            k2 = qkv_ref[rows, DN_W + p * W2:DN_W + (p + 1) * W2].astype(F32)
            kb = k2 * betaw
            qs = qkv_ref[rows, p * W2:(p + 1) * W2].astype(F32) * qscale
            ecolw = jnp.exp(colw)
            vbeta = qkv_ref[rows, 2 * DN_W + p * W2:2 * DN_W + (p + 1) * W2].astype(F32) * betaw
            kbe = kb * ecolw
            ch.append(dict(
                d=d, p=p, strict=strict, rows=rows, si=(bb * 2 + d) * (H_DN // 2) + p,
                dec=jnp.exp(jnp.where(incl, col2 - row2, -jnp.inf)),
                lhs=jnp.concatenate([kb.astype(BF16), qs.astype(BF16)], 0),
                kbd=bdw(k2),
                qg=(qs * ecolw).astype(BF16),
                kd=(k2 * jnp.exp(jnp.where(firstw, gl_a, gl_b) - colw)).astype(BF16),
                rhs=jnp.concatenate([
                    jnp.concatenate([vbeta[:, 0:DK], kbe[:, 0:DK]], 1),
                    jnp.concatenate([vbeta[:, DK:W2], kbe[:, DK:W2]], 1)], 0).astype(BF16),
                gls=jnp.where(sr, jnp.exp(gl_a), jnp.exp(gl_b)),
            ))

    for c in ch:
        c["kq"] = lax.dot_general(c["lhs"], c["kbd"], nt, preferred_element_type=F32)
    for c in ch:
        kq = c["kq"]
        nmat = -jnp.where(c["strict"], kq[0:C] * c["dec"], 0.0)
        c["attn"] = (kq[C:2 * C] * c["dec"]).astype(BF16)
        c["x"] = eye2 + nmat
        c["nm"] = nmat
    for c in ch:
        c["pw"] = _bdot(c["nm"].astype(BF16), bd(c["nm"]))
    for it in range(5):
        for c in ch:
            c["pbd"] = bd(c["pw"])
        for c in ch:
            c["x"] = c["x"] + _bdot(c["x"].astype(BF16), c["pbd"])
        if it < 4:
            for c in ch:
                c["pw"] = _bdot(c["pw"].astype(BF16), c["pbd"])
    for c in ch:
        x = c["x"]
        xl = jnp.concatenate([jnp.where(first, x, 0.0), jnp.where(first, 0.0, x)], 0).astype(BF16)
        uw = _bdot(xl, c["rhs"])
        c["u"] = jnp.concatenate([uw[0:C, 0:DK], uw[C:2 * C, 0:DK]], 1)
        w = jnp.concatenate([uw[0:C, DK:W2], uw[C:2 * C, DK:W2]], 1)
        c["wq"] = jnp.concatenate([w.astype(BF16), c["qg"]], 0)
    for c in ch:
        c["s"] = s_ref[c["si"]]
        c["wqs"] = _bdot(c["wq"], c["s"].astype(BF16))
    for c in ch:
        c["vn"] = c["u"] - c["wqs"][0:C]
    for c in ch:
        o = c["wqs"][C:2 * C] + _bdot(c["attn"], bdw(c["vn"]))
        o_ref = (o0_ref, o1_ref)[c["d"]]
        o_ref[c["rows"], c["p"] * W2:(c["p"] + 1) * W2] = o.astype(o_ref.dtype)
    for c in ch:
        kv = lax.dot_general(c["kd"], c["vn"].astype(BF16), tn, preferred_element_type=F32)
        s_ref[c["si"]] = c["s"] * c["gls"] + jnp.where(sdiag, kv, 0.0)


def _dn_group(i):
    b = i // DN_XG
    g = i % DN_XG
    return jnp.where(i < MX // DN_TT, ((b // DN_NB) * DN_XG + g) * DN_NB + b % DN_NB, i)


def _dnscan(qkv, gb):
    cblk = MX // (DN_NB * DN_TT)
    rows = DN_NB * DN_TT

    def idx0(bp, s):
        return jnp.where(s == 0, cblk + bp, bp * DN_XG + s - 1)

    def idx1(bp, s):
        return jnp.where(s == 0, cblk + bp, bp * DN_XG + DN_XG - s)

    n_state = DN_NB * 2 * (H_DN // 2)
    est = 2 * 2 * (rows * 1536 * 2 + rows * 128 * 4 + rows * 512 * 4) + n_state * 4 * DK * DK * 4 + (4 << 20)
    return pl.pallas_call(
        _dnscan_kernel,
        grid=(B // DN_NB, 1 + DN_XG),
        in_specs=[
            pl.BlockSpec((rows, 3 * DN_W), lambda bp, s: (idx0(bp, s), 0)),
            pl.BlockSpec((rows, 3 * DN_W), lambda bp, s: (idx1(bp, s), 0)),
            pl.BlockSpec((rows, 128), lambda bp, s: (idx0(bp, s), 0)),
            pl.BlockSpec((rows, 128), lambda bp, s: (idx1(bp, s), 0)),
        ],
        out_specs=[
            pl.BlockSpec((rows, DN_W), lambda bp, s: (idx0(bp, s), 0)),
            pl.BlockSpec((rows, DN_W), lambda bp, s: (idx1(bp, s), 0)),
        ],
        out_shape=[jax.ShapeDtypeStruct((M, DN_W), F32), jax.ShapeDtypeStruct((M, DN_W), F32)],
        scratch_shapes=[pltpu.VMEM((n_state, 2 * DK, 2 * DK), F32)],
        compiler_params=_cparams(("arbitrary", "arbitrary"), est),
        name="dn_scan",
    )(qkv, qkv, gb, gb)


def _mlaproj_kernel(cq_ref, ckv_ref, sm_ref, cos_ref, sin_ref, gq_ref, gkv_ref,
                    wq1_ref, wq2_ref, wk_ref, wv_ref, q_ref, k_ref, v_ref):
    cq = cq_ref[...].astype(F32)
    hq = (cq * lax.rsqrt(jnp.sum(cq * cq, -1, keepdims=True) * (1.0 / Q_RANK) + EPS) * gq_ref[...]).astype(BF16)
    ckv = ckv_ref[...].astype(F32)
    hkv = (ckv * lax.rsqrt(jnp.mean(ckv * ckv, -1, keepdims=True) + EPS) * gkv_ref[...]).astype(BF16)
    q1 = _bdot(hq, wq1_ref[...])
    q2 = _bdot(hq, wq2_ref[...])
    kn = _bdot(hkv, wk_ref[...])
    vv = _bdot(hkv, wv_ref[...])
    cosq = cos_ref[...]
    sin = sin_ref[...]
    lane = lax.broadcasted_iota(jnp.int32, cosq.shape, 1)
    cosk = jnp.where(lane >= NOPE, cosq, 0.0)
    kpe = sm_ref[:, 0:HP] * cosk + sm_ref[:, HP:2 * HP] * sin
    for h in range(H_MLA):
        sl = slice(h * HP, (h + 1) * HP)
        q_ref[:, sl] = ((q1[:, sl] * cosq + q2[:, sl] * sin) * QK_SCALE).astype(BF16)
        k_ref[:, sl] = (kn[:, sl] + kpe).astype(BF16)
        v_ref[:, sl] = jnp.where(lane == V_DIM, 1.0, vv[:, sl]).astype(BF16)


def _mlaproj(u, usm, cos_t, sin_t, gq, gkv, wq1, wq2, wk, wv):
    tm = 256
    wide = H_MLA * HP
    est = 2 * (tm * (CQ_PAD + KV_RANK) * 2 + tm * 256 * 4 + 2 * tm * 128 * 4 + 2 * CQ_PAD * wide * 2
               + 2 * KV_RANK * wide * 2 + 3 * tm * wide * 2) + 6 * tm * wide * 4
    full = lambda shape: pl.BlockSpec(shape, lambda i: (0, 0))
    return pl.pallas_call(
        _mlaproj_kernel,
        grid=(M // tm,),
        in_specs=[
            pl.BlockSpec((tm, CQ_PAD), lambda i: (i, C_CQ // CQ_PAD)),
            pl.BlockSpec((tm, KV_RANK), lambda i: (i, C_CKV // KV_RANK)),
            pl.BlockSpec((tm, N_SMALL), lambda i: (i, 0)),
            pl.BlockSpec((tm, HP), lambda i: (i, 0)),
            pl.BlockSpec((tm, HP), lambda i: (i, 0)),
            full((1, CQ_PAD)), full((1, KV_RANK)),
            full((CQ_PAD, wide)), full((CQ_PAD, wide)), full((KV_RANK, wide)), full((KV_RANK, wide)),
        ],
        out_specs=[pl.BlockSpec((tm, wide), lambda i: (i, 0))] * 3,
        out_shape=[jax.ShapeDtypeStruct((M, wide), BF16)] * 3,
        compiler_params=_cparams(("arbitrary",), est),
        name="mla_proj",
    )(u, u, usm, cos_t, sin_t, gq, gkv, wq1, wq2, wk, wv)


ATT_TQ = 256
ATT_TK = 512
ATT_HG = 2


def _attn_kernel(q_ref, kx_ref, vx_ref, kc_ref, vc_ref, o_ref):
    qi = pl.program_id(2)
    nt = (((1,), (1,)), ((), ()))
    heads = [slice(g * HP, (g + 1) * HP) for g in range(ATT_HG)]
    qs = [q_ref[:, sl] for sl in heads]

    def scores(g, k_ref, rows):
        return lax.dot_general(qs[g], k_ref[rows, heads[g]], nt, preferred_element_type=F32)

    def update(carry, s, v_blk):
        m, acc = carry
        m_new = jnp.maximum(m, jnp.max(s, -1, keepdims=True))
        p = jnp.exp(s - m_new).astype(BF16)
        return m_new, jnp.exp(m - m_new) * acc + _bdot(p, v_blk)

    def finish(carries):
        for g, (_, acc) in enumerate(carries):
            o_ref[:, heads[g]] = (acc / acc[:, V_DIM:V_DIM + 1]).astype(o_ref.dtype)

    init = (jnp.full((ATT_TQ, 1), -jnp.inf, F32), jnp.zeros((ATT_TQ, HP), F32))
    everything = slice(None)

    def ctx_block():
        return [update(init, scores(g, kc_ref, everything), vc_ref[:, heads[g]]) for g in range(ATT_HG)]

    @pl.when(qi < T // ATT_TQ)
    def _():
        nblk = T // ATT_TK
        s_next = [scores(g, kx_ref, slice(0, ATT_TK)) for g in range(ATT_HG)]
        carries = ctx_block()
        for j in range(nblk):
            s = s_next
            if j + 1 < nblk:
                s_next = [scores(g, kx_ref, slice((j + 1) * ATT_TK, (j + 2) * ATT_TK)) for g in range(ATT_HG)]
            rows = slice(j * ATT_TK, (j + 1) * ATT_TK)
            carries = [update(carries[g], s[g], vx_ref[rows, heads[g]]) for g in range(ATT_HG)]
        finish(carries)

    @pl.when(qi >= T // ATT_TQ)
    def _():
        finish(ctx_block())


def _attention(q, k, v, with_ctx_queries):
    tq = ATT_TQ
    nqx = T // tq
    nq = nqx + (1 if with_ctx_queries else 0)
    cblk = MX // CTX
    rows = M if with_ctx_queries else MX

    def qidx(b, h, i):
        return (jnp.where(i < nqx, b * nqx + i, MX // tq + b), h)

    wide = ATT_HG * HP
    est = 2 * (2 * T * wide * 2 + 2 * CTX * wide * 2 + 2 * tq * wide * 2) + 8 * ATT_HG * tq * ATT_TK * 4
    return pl.pallas_call(
        _attn_kernel,
        grid=(B, H_MLA // ATT_HG, nq),
        in_specs=[
            pl.BlockSpec((tq, wide), qidx),
            pl.BlockSpec((T, wide), lambda b, h, i: (b, h)),
            pl.BlockSpec((T, wide), lambda b, h, i: (b, h)),
            pl.BlockSpec((CTX, wide), lambda b, h, i: (cblk + b, h)),
            pl.BlockSpec((CTX, wide), lambda b, h, i: (cblk + b, h)),
        ],
        out_specs=pl.BlockSpec((tq, wide), qidx),
        out_shape=jax.ShapeDtypeStruct((rows, H_MLA * HP), BF16),
        compiler_params=_cparams(("arbitrary", "arbitrary", "arbitrary"), est),
        name="mla_attention",
    )(q, k, v, k, v)


def _mm1_kernel(a_ref, b_ref, o_ref):
    o_ref[...] = _bdot(a_ref[...], b_ref[...]).astype(o_ref.dtype)


def _fourier_channels(u, w_ab):
    tm = 512
    est = 2 * (tm * F_W * 2 + F_W * 2 * F_W * 2 + tm * 2 * F_W * 2)
    return pl.pallas_call(
        _mm1_kernel,
        grid=(M // tm,),
        in_specs=[pl.BlockSpec((tm, F_W), lambda i: (i, C_F // F_W)),
                  pl.BlockSpec((F_W, 2 * F_W), lambda i: (0, 0))],
        out_specs=pl.BlockSpec((tm, 2 * F_W), lambda i: (i, 0)),
        out_shape=jax.ShapeDtypeStruct((M, 2 * F_W), BF16),
        compiler_params=_cparams(("arbitrary",), est),
        name="fourier_channels",
    )(u, w_ab)


def _dft_kernel(cs_ref, ab_ref, o_ref, acc_ref, *, nk):
    kk = pl.program_id(2)

    @pl.when(kk == 0)
    def _():
        acc_ref[...] = jnp.zeros_like(acc_ref)

    acc_ref[...] += _bdot(cs_ref[...], ab_ref[...])

    @pl.when(kk == nk - 1)
    def _():
        o_ref[...] = acc_ref[...].astype(o_ref.dtype)


def _fourier_tokens(cs, ab, seq, row0, tm, tk):
    nk = 2 * seq // tk
    kpb = seq // tk
    est = 2 * (tm * tk * 2 + tk * F_W * 2 + tm * F_W * 2) + tm * F_W * 4
    return pl.pallas_call(
        functools.partial(_dft_kernel, nk=nk),
        grid=(B, seq // tm, nk),
        in_specs=[
            pl.BlockSpec((tm, tk), lambda b, i, kk: (i, kk)),
            pl.BlockSpec((tk, F_W), lambda b, i, kk: (row0 // tk + b * kpb + kk % kpb, kk // kpb)),
        ],
        out_specs=pl.BlockSpec((tm, F_W), lambda b, i, kk: (b * (seq // tm) + i, 0)),
        out_shape=jax.ShapeDtypeStruct((B * seq, F_W), BF16),
        scratch_shapes=[pltpu.VMEM((tm, F_W), F32)],
        compiler_params=_cparams(("arbitrary", "arbitrary", "arbitrary"), est),
        name="fourier_tokens",
    )(cs, ab)


def _merge_kernel(x_ref, mod_ref, o0_ref, o1_ref, z_ref, yb_ref, ycx_ref, ycc_ref, gt_ref, gdn_ref,
                  wa_ref, wb_ref, wc_ref, wo_ref, out_ref, *, x_tiles):
    yc = jnp.where(pl.program_id(0) < x_tiles, ycx_ref[...], ycc_ref[...])
    o = o0_ref[...] + o1_ref[...]
    z = z_ref[...].astype(F32)
    gdn = gdn_ref[...]
    ya = []
    for h in range(H_DN):
        sl = slice(h * DK, (h + 1) * DK)
        oh = o[:, sl]
        zh = z[:, sl]
        yh = oh * lax.rsqrt(jnp.mean(oh * oh, -1, keepdims=True) + EPS) * gdn
        ya.append((yh * (zh * jax.nn.sigmoid(zh))).astype(BF16))
    ya = jnp.concatenate(ya, -1)
    s = jax.nn.sigmoid(gt_ref[:, 0:D].astype(F32)) * _bdot(ya, wa_ref[...])
    s = s + jax.nn.sigmoid(gt_ref[:, D:2 * D].astype(F32)) * _bdot(yb_ref[...], wb_ref[...])
    s = s + jax.nn.sigmoid(gt_ref[:, 2 * D:3 * D].astype(F32)) * _bdot(yc, wc_ref[...])
    mix = _bdot(s.astype(BF16), wo_ref[...])
    out_ref[...] = x_ref[...] + mod_ref[2:3, :] * mix


def _merge(X, mod, o0, o1, u, yb, ycx, ycc, gdn, wa, wb, wc, wo, rows):
    tm = 256
    x_tiles = MX // tm
    full = lambda shape: pl.BlockSpec(shape, lambda i: (0, 0))
    est = 2 * (2 * tm * D * 4 + 2 * tm * DN_W * 4 + tm * DN_W * 2 * 3 + tm * D * 2 + tm * 3 * D * 2
               + (2 * DN_W + 2 * D) * D * 2) + 6 * tm * D * 4
    return pl.pallas_call(
        functools.partial(_merge_kernel, x_tiles=x_tiles),
        grid=(rows // tm,),
        in_specs=[
            pl.BlockSpec((tm, D), lambda i: (i, 0)),
            _mod_spec(tm),
            pl.BlockSpec((tm, DN_W), lambda i: (_dn_group(i), 0)),
            pl.BlockSpec((tm, DN_W), lambda i: (_dn_group(i), 0)),
            pl.BlockSpec((tm, DN_W), lambda i: (i, C_Z // DN_W)),
            pl.BlockSpec((tm, H_MLA * HP), lambda i: (i, 0)),
            pl.BlockSpec((tm, F_W), lambda i: (jnp.minimum(i, x_tiles - 1), 0)),
            pl.BlockSpec((tm, F_W), lambda i: (jnp.maximum(i - x_tiles, 0), 0)),
            pl.BlockSpec((tm, 3 * D), lambda i: (i, 0)),
            full((1, DK)),
            full((DN_W, D)), full((H_MLA * HP, D)), full((F_W, D)), full((D, D)),
        ],
        out_specs=pl.BlockSpec((tm, D), lambda i: (i, 0)),
        out_shape=jax.ShapeDtypeStruct((rows, D), F32),
        compiler_params=_cparams(("arbitrary",), est),
        name="merge",
    )(X, mod, o0, o1, u, yb, ycx, ycc, u, gdn, wa, wb, wc, wo)


def _ffn_kernel(x_ref, g_ref, mod_ref, wg_ref, wu_ref, wd_ref, o_ref, h_ref, acc_ref, *, nf):
    f = pl.program_id(1)

    @pl.when(f == 0)
    def _():
        h_ref[...] = _norm_mod(x_ref[...], g_ref[...], mod_ref[3:4, :], mod_ref[4:5, :]).astype(BF16)
        acc_ref[...] = jnp.zeros_like(acc_ref)

    h = h_ref[...]
    a = _bdot(h, wg_ref[...])
    up = _bdot(h, wu_ref[...])
    act = (a * jax.nn.sigmoid(a) * up).astype(BF16)
    acc_ref[...] += _bdot(act, wd_ref[...])

    @pl.when(f == nf - 1)
    def _():
        o_ref[...] = x_ref[...] + mod_ref[5:6, :] * acc_ref[...]


def _ffn(X, gain, mod, wg, wu, wd):
    tm, tf = 512, 1408
    nf = D_FF // tf
    est = 2 * (2 * tm * D * 4 + 3 * D * tf * 2) + tm * D * 6 + 3 * tm * tf * 4
    return pl.pallas_call(
        functools.partial(_ffn_kernel, nf=nf),
        grid=(M // tm, nf),
        in_specs=[
            pl.BlockSpec((tm, D), lambda i, f: (i, 0)),
            pl.BlockSpec((1, D), lambda i, f: (0, 0)),
            _mod_spec(tm),
            pl.BlockSpec((D, tf), lambda i, f: (0, f)),
            pl.BlockSpec((D, tf), lambda i, f: (0, f)),
            pl.BlockSpec((tf, D), lambda i, f: (f, 0)),
        ],
        out_specs=pl.BlockSpec((tm, D), lambda i, f: (i, 0)),
        out_shape=jax.ShapeDtypeStruct((M, D), F32),
        scratch_shapes=[pltpu.VMEM((tm, D), BF16), pltpu.VMEM((tm, D), F32)],
        compiler_params=_cparams(("arbitrary", "arbitrary"), est),
        name="ffn_dense",
    )(X, gain, mod, wg, wu, wd)


def _router_kernel(x_ref, g_ref, mod_ref, wr_ref, h_ref, r_ref):
    h = _norm_mod(x_ref[...], g_ref[...], mod_ref[3:4, :], mod_ref[4:5, :])
    h_ref[...] = h
    h_hi = h.astype(BF16)
    h_lo = (h - h_hi.astype(F32)).astype(BF16)
    wr = wr_ref[...]
    w_hi = wr.astype(BF16)
    w_lo = (wr - w_hi.astype(F32)).astype(BF16)
    logits = _bdot(h_hi, w_hi) + _bdot(h_lo, w_hi) + _bdot(h_hi, w_lo)
    lane = lax.broadcasted_iota(jnp.int32, logits.shape, 1)
    logits = jnp.where(lane < N_EXP, logits, -jnp.inf)
    m1 = jnp.max(logits, -1, keepdims=True)
    i1 = jnp.min(jnp.where(logits == m1, lane, 128), -1, keepdims=True)
    rest = jnp.where(lane == i1, -jnp.inf, logits)
    m2 = jnp.max(rest, -1, keepdims=True)
    i2 = jnp.min(jnp.where(rest == m2, lane, 128), -1, keepdims=True)
    e2 = jnp.exp(m2 - m1)
    w1 = 1.0 / (1.0 + e2)
    w2 = e2 / (1.0 + e2)
    r = jnp.where(lane == 0, i1.astype(F32), 0.0)
    r = jnp.where(lane == 1, i2.astype(F32), r)
    r = jnp.where(lane == 2, w1, r)
    r = jnp.where(lane == 3, w2, r)
    r_ref[...] = r


def _router(X, gain, mod, wr_pad):
    tm = 512
    est = 2 * (tm * D * 4 + D * 128 * 4 + tm * D * 2 + tm * 128 * 4) + 4 * tm * D * 4
    return pl.pallas_call(
        _router_kernel,
        grid=(MX // tm,),
        in_specs=[
            pl.BlockSpec((tm, D), lambda i: (i, 0)),
            pl.BlockSpec((1, D), lambda i: (0, 0)),
            _mod_spec(tm),
            pl.BlockSpec((D, 128), lambda i: (0, 0)),
        ],
        out_specs=[pl.BlockSpec((tm, D), lambda i: (i, 0)), pl.BlockSpec((tm, 128), lambda i: (i, 0))],
        out_shape=[jax.ShapeDtypeStruct((MX, D), F32), jax.ShapeDtypeStruct((MX, 128), F32)],
        compiler_params=_cparams(("arbitrary",), est),
        name="moe_router",
    )(X, gain, mod, wr_pad)


MOE_TM = 512
MOE_TF = 1792
MOE_ROWS = MX * TOP_K + N_EXP * MOE_TM
MOE_TILES = MOE_ROWS // MOE_TM


def _moe_kernel(te_ref, nt_ref, tok_ref, h_hbm, rw_ref, wg_ref, wu_ref, wd_ref, o_ref,
                hbuf, sem, hb_ref, acc_ref, *, nf):
    t = pl.program_id(0)
    f = pl.program_id(1)
    n_tiles = nt_ref[0]
    valid = t < n_tiles

    def row_copy(tile, r, slot):
        return pltpu.make_async_copy(
            h_hbm.at[pl.ds(tok_ref[tile * MOE_TM + r], 1)], hbuf.at[slot, pl.ds(r, 1)], sem.at[slot])

    def start_tile(tile, slot):
        def body(r, carry):
            row_copy(tile, r, slot).start()
            return carry
        lax.fori_loop(0, MOE_TM, body, 0, unroll=8)

    def wait_tile(tile, slot):
        def body(r, carry):
            row_copy(tile, r, slot).wait()
            return carry
        lax.fori_loop(0, MOE_TM, body, 0, unroll=8)

    @pl.when(jnp.logical_and(valid, jnp.logical_and(f == 0, t == 0)))
    def _():
        start_tile(0, 0)

    @pl.when(jnp.logical_and(valid, f == 0))
    def _():
        slot = t % 2

        @pl.when(t + 1 < n_tiles)
        def _():
            start_tile(t + 1, 1 - slot)

        wait_tile(t, slot)
        hb_ref[...] = hbuf[slot].astype(BF16)
        acc_ref[...] = jnp.zeros_like(acc_ref)

    @pl.when(valid)
    def _():
        h = hb_ref[...]
        a = _bdot(h, wg_ref[...])
        up = _bdot(h, wu_ref[...])
        act = (a * jax.nn.sigmoid(a) * up).astype(BF16)
        acc_ref[...] += _bdot(act, wd_ref[...])

    @pl.when(jnp.logical_and(valid, f == nf - 1))
    def _():
        o_ref[...] = (acc_ref[...] * rw_ref[...]).astype(o_ref.dtype)

    @pl.when(jnp.logical_and(jnp.logical_not(valid), f == nf - 1))
    def _():
        o_ref[...] = jnp.zeros_like(o_ref)


def _moe_experts(tile_expert, n_tiles, row_token, h, row_w, wg, wu, wd):
    tm, tf = MOE_TM, MOE_TF
    nf = E_FF // tf

    def fsel(t, f, nt):
        return jnp.where(t < nt[0], f, nf - 1)

    est = 2 * (tm * 128 * 4 + 3 * D * tf * 2 + tm * D * 2) + 2 * tm * D * 4 + tm * D * 6 + 3 * tm * tf * 4
    grid_spec = pltpu.PrefetchScalarGridSpec(
        num_scalar_prefetch=3,
        grid=(MOE_TILES, nf),
        in_specs=[
            pl.BlockSpec(memory_space=pl.ANY),
            pl.BlockSpec((tm, 1), lambda t, f, te, nt, tok: (t, 0)),
            pl.BlockSpec((None, D, tf), lambda t, f, te, nt, tok: (te[t], 0, fsel(t, f, nt))),
            pl.BlockSpec((None, D, tf), lambda t, f, te, nt, tok: (te[t], 0, fsel(t, f, nt))),
            pl.BlockSpec((None, tf, D), lambda t, f, te, nt, tok: (te[t], fsel(t, f, nt), 0)),
        ],
        out_specs=pl.BlockSpec((tm, D), lambda t, f, te, nt, tok: (t, 0)),
        scratch_shapes=[pltpu.VMEM((2, tm, D), F32), pltpu.SemaphoreType.DMA((2,)),
                        pltpu.VMEM((tm, D), BF16), pltpu.VMEM((tm, D), F32)],
    )
    return pl.pallas_call(
        functools.partial(_moe_kernel, nf=nf),
        grid_spec=grid_spec,
        out_shape=jax.ShapeDtypeStruct((MOE_ROWS, D), BF16),
        compiler_params=_cparams(("arbitrary", "arbitrary"), est),
        name="moe_experts",
    )(tile_expert, n_tiles, row_token, h, row_w, wg, wu, wd)


def _final_kernel(x_ref, mod_ref, y0_ref, y1_ref, g_ref, o_ref):
    x = x_ref[...] + mod_ref[5:6, :] * (y0_ref[...].astype(F32) + y1_ref[...].astype(F32))
    o_ref[...] = x * lax.rsqrt(jnp.mean(x * x, -1, keepdims=True) + EPS) * g_ref[...]


def _final(X, mod, y0, y1, gain):
    tm = 512
    est = 2 * (2 * tm * D * 4 + 2 * tm * D * 2) + 3 * tm * D * 4
    return pl.pallas_call(
        _final_kernel,
        grid=(MX // tm,),
        in_specs=[
            pl.BlockSpec((tm, D), lambda i: (i, 0)),
            _mod_spec(tm),
            pl.BlockSpec((tm, D), lambda i: (i, 0)),
            pl.BlockSpec((tm, D), lambda i: (i, 0)),
            pl.BlockSpec((1, D), lambda i: (0, 0)),
        ],
        out_specs=pl.BlockSpec((tm, D), lambda i: (i, 0)),
        out_shape=jax.ShapeDtypeStruct((MX, D), F32),
        compiler_params=_cparams(("arbitrary",), est),
        name="final_norm",
    )(X, mod, y0, y1, gain)


def _moe_routing(route):
    tm = MOE_TM
    e_flat = route[:, 0:TOP_K].astype(jnp.int32).reshape(-1)
    w_flat = route[:, TOP_K:2 * TOP_K].reshape(-1)
    n = e_flat.shape[0]
    eids = jnp.arange(N_EXP, dtype=jnp.int32)[None, :]

    def pick(onehot, table):
        return jnp.sum(onehot * table[None, :], 1)

    order = jnp.argsort(e_flat, stable=True).astype(jnp.int32)
    inv_order = jnp.argsort(order).astype(jnp.int32)
    oh_f = (e_flat[:, None] == eids).astype(jnp.int32)
    counts = jnp.sum(oh_f, 0)
    padded = ((counts + tm - 1) // tm) * tm
    start = jnp.cumsum(counts) - counts
    pend = jnp.cumsum(padded)
    pstart = pend - padded
    slot_row = pick(oh_f, pstart - start) + inv_order

    n_tiles = (pend[-1] // tm).astype(jnp.int32)
    tile_start = jnp.arange(MOE_TILES, dtype=jnp.int32) * tm
    tile_expert = jnp.minimum(jnp.sum(tile_start[:, None] >= pend[None, :], 1), N_EXP - 1).astype(jnp.int32)
    rows = jnp.arange(MOE_ROWS, dtype=jnp.int32)
    oh_r = (jnp.repeat(tile_expert, tm)[:, None] == eids).astype(jnp.int32)
    rank_r = rows - pick(oh_r, pstart)
    valid = jnp.logical_and(rank_r < pick(oh_r, counts), rows < pend[-1])
    slot_r = order[jnp.clip(pick(oh_r, start) + rank_r, 0, n - 1)]
    row_token = jnp.where(valid, slot_r // TOP_K, 0)
    row_w = jnp.where(valid, w_flat[slot_r], 0.0)
    last_expert = jnp.sum(jnp.where(jnp.arange(MOE_TILES) == n_tiles - 1, tile_expert, 0))
    tile_expert = jnp.where(tile_start < pend[-1], tile_expert, last_expert)
    return row_token, row_w.reshape(MOE_ROWS, 1), slot_row.reshape(MX, TOP_K), tile_expert, n_tiles.reshape(1)


def _dft_table(n):
    split = 64 if n > 64 else 1
    t = jnp.arange(n, dtype=jnp.int32)[None, :]
    a = jnp.arange(n // split, dtype=jnp.int32)[:, None]
    b = jnp.arange(split, dtype=jnp.int32)[:, None]
    na = n // split
    ang_a = ((a * t) % na).astype(F32) * (2.0 * np.pi / na)
    ang_b = ((b * t) % n).astype(F32) * (2.0 * np.pi / n)
    scale = n ** -0.5
    ca, sa = jnp.cos(ang_a)[:, None, :], jnp.sin(ang_a)[:, None, :]
    cb, sb = (jnp.cos(ang_b) * scale)[None, :, :], (jnp.sin(ang_b) * scale)[None, :, :]
    c = (ca * cb - sa * sb).reshape(n, n)
    s = (sa * cb + ca * sb).reshape(n, n)
    return jnp.concatenate([c, s], 1).astype(BF16)


def _channel_table():
    k = jnp.arange(F_GDIM, dtype=jnp.int32)
    ang = ((k[:, None] * k[None, :]) % F_GDIM).astype(F32) * (2.0 * np.pi / F_GDIM)
    scale = F_GDIM ** -0.5
    eye = jnp.eye(F_GROUPS, dtype=F32)
    return jnp.concatenate([jnp.kron(eye, jnp.cos(ang) * scale), jnp.kron(eye, -jnp.sin(ang) * scale)], 1).astype(BF16)


def _rope_tables():
    t = jnp.arange(T, dtype=jnp.int32)
    r = (t // GRID_W).astype(F32)[:, None]
    c = (t % GRID_W).astype(F32)[:, None]
    axis_dim = ROPE // 2
    inv = ROPE_BASE ** (-jnp.arange(0, axis_dim, 2, dtype=F32) / axis_dim)[None, :]
    cr, sr, cc, sc = jnp.cos(r * inv), jnp.sin(r * inv), jnp.cos(c * inv), jnp.sin(c * inv)
    cos32 = jnp.concatenate([cr, cr, cc, cc], 1)
    sin32 = jnp.concatenate([-sr, sr, -sc, sc], 1)
    ones = jnp.ones((T, NOPE), F32)
    zeros = jnp.zeros((T, NOPE), F32)
    tail = jnp.zeros((T, HP - NOPE - ROPE), F32)
    cos_x = jnp.concatenate([ones, cos32, tail], 1)
    sin_x = jnp.concatenate([zeros, sin32, tail], 1)
    cos_c = jnp.concatenate([jnp.ones((MC, NOPE + ROPE), F32), jnp.zeros((MC, HP - NOPE - ROPE), F32)], 1)
    cos_t = jnp.concatenate([jnp.tile(cos_x, (B, 1)), cos_c], 0)
    sin_t = jnp.concatenate([jnp.tile(sin_x, (B, 1)), jnp.zeros((MC, HP), F32)], 0)
    return cos_t, sin_t


_ROPE_SWAP = np.concatenate([np.arange(8, 16), np.arange(0, 8), np.arange(24, 32), np.arange(16, 24)])


def _layer_weights(w_in, w_qup, w_kvup, w_branch):
    sizes = (DN_W, DN_W, DN_W, DN_W, 2 * H_DN, 2 * H_DN, Q_RANK, KV_RANK, ROPE, F_W, 3 * D)
    offs = np.concatenate([[0], np.cumsum(sizes)])
    seg = [w_in[:, offs[i]:offs[i + 1]] for i in range(len(sizes))]
    dq, dk, dv, dz, da, db, cq, ckv, kpe, fo, gates = seg
    zc = lambda n: jnp.zeros((D, n), F32)
    w_main = jnp.concatenate([gates, dq, dk, dv, dz, fo, cq, zc(CQ_PAD - Q_RANK), ckv], 1).astype(BF16)
    w_small = jnp.concatenate([da, db, zc(NOPE - 4 * H_DN), kpe, zc(HP - NOPE - ROPE),
                               zc(NOPE), kpe[:, _ROPE_SWAP], zc(HP - NOPE - ROPE)], 1).astype(BF16)

    qh = w_qup.reshape(Q_RANK, H_MLA, NOPE + ROPE)
    zq = lambda n: jnp.zeros((Q_RANK, H_MLA, n), F32)
    wq1 = jnp.concatenate([qh, zq(HP - NOPE - ROPE)], 2).reshape(Q_RANK, H_MLA * HP)
    wq2 = jnp.concatenate([zq(NOPE), qh[:, :, NOPE:][:, :, _ROPE_SWAP], zq(HP - NOPE - ROPE)], 2).reshape(Q_RANK, H_MLA * HP)
    rowpad = jnp.zeros((CQ_PAD - Q_RANK, H_MLA * HP), F32)
    wq1 = jnp.concatenate([wq1, rowpad], 0).astype(BF16)
    wq2 = jnp.concatenate([wq2, rowpad], 0).astype(BF16)
    kvh = w_kvup.reshape(KV_RANK, H_MLA, NOPE + V_DIM)
    zk = jnp.zeros((KV_RANK, H_MLA, HP - NOPE), F32)
    wk = jnp.concatenate([kvh[:, :, :NOPE], zk], 2).reshape(KV_RANK, H_MLA * HP).astype(BF16)
    wv = jnp.concatenate([kvh[:, :, NOPE:], zk], 2).reshape(KV_RANK, H_MLA * HP).astype(BF16)

    wb = w_branch[1].reshape(H_MLA, V_DIM, D)
    wb = jnp.concatenate([wb, jnp.zeros((H_MLA, HP - V_DIM, D), F32)], 1).reshape(H_MLA * HP, D)
    return dict(w_main=w_main, w_small=w_small, wq1=wq1, wq2=wq2, wk=wk, wv=wv,
                wa=w_branch[0].astype(BF16), wb=wb.astype(BF16), wc=w_branch[2].astype(BF16))


def kernel(x, c, ctx, c_ctx, w_mod, b_mod, norm_mix, norm_ffn, w_in, dn_conv, dn_a_log, dn_dt_bias, dn_norm, mla_q_norm, mla_kv_norm, mla_w_qup, mla_w_kvup, w_branch, w_out, ffn_w_gate, ffn_w_up, ffn_w_down, moe_router, moe_w_gate, moe_w_up, moe_w_down, final_norm):
    X = jnp.concatenate([x.reshape(MX, D), ctx.reshape(MC, D)], 0)
    mods = _mod_vectors(c, c_ctx, w_mod, b_mod)
    cos_t, sin_t = _rope_tables()
    cs_x = _dft_table(T)
    cs_c = _dft_table(CTX)
    w_ab = _channel_table()

    out = None
    for l in range(DEPTH):
        last = l == DEPTH - 1
        mod = mods[l]
        lw = _layer_weights(w_in[l], mla_w_qup[l], mla_w_kvup[l], w_branch[l])
        u, usm = _inproj(X, norm_mix[l].reshape(1, D), mod, lw["w_main"], lw["w_small"])
        qkv, gb = _dnpre(u, usm, dn_conv[l], dn_a_log[l], dn_dt_bias[l])
        o0, o1 = _dnscan(qkv, gb)
        gq = jnp.concatenate([mla_q_norm[l], jnp.zeros((CQ_PAD - Q_RANK,), F32)]).reshape(1, CQ_PAD)
        q, k, v = _mlaproj(u, usm, cos_t, sin_t, gq, mla_kv_norm[l].reshape(1, KV_RANK),
                           lw["wq1"], lw["wq2"], lw["wk"], lw["wv"])
        yb = _attention(q, k, v, with_ctx_queries=not last)
        ab = _fourier_channels(u, w_ab)
        ycx = _fourier_tokens(cs_x, ab, T, 0, 1024, 2048)
        ycc = ycx if last else _fourier_tokens(cs_c, ab, CTX, MX, CTX, CTX)
        rows = MX if last else M
        X = _merge(X, mod, o0, o1, u, yb, ycx, ycc, dn_norm[l].reshape(1, DK),
                   lw["wa"], lw["wb"], lw["wc"], w_out[l].astype(BF16), rows)
        e = l // 2
        if l % 2 == 0:
            X = _ffn(X, norm_ffn[l].reshape(1, D), mod, ffn_w_gate[e].astype(BF16),
                     ffn_w_up[e].astype(BF16), ffn_w_down[e].astype(BF16))
        else:
            wr = jnp.concatenate([moe_router[e], jnp.zeros((D, 128 - N_EXP), F32)], 1)
            h, route = _router(X, norm_ffn[l].reshape(1, D), mod, wr)
            row_token, row_w, slot_row, tile_expert, n_tiles = _moe_routing(route)
            y = _moe_experts(tile_expert, n_tiles, row_token, h, row_w, moe_w_gate[e].astype(BF16),
                             moe_w_up[e].astype(BF16), moe_w_down[e].astype(BF16))
            y0 = jnp.take(y, slot_row[:, 0], axis=0)
            y1 = jnp.take(y, slot_row[:, 1], axis=0)
            out = _final(X, mod, y0, y1, final_norm.reshape(1, D))
    return out.reshape(B, T, D)
```

```python
import functools

import numpy as np
import jax
import jax.numpy as jnp
from jax import lax
from jax.experimental import pallas as pl
from jax.experimental.pallas import tpu as pltpu

F32 = jnp.float32
BF16 = jnp.bfloat16

D = 1024
B = 4
T = 4096
CTX = 256
DEPTH = 2
GRID_W = 64
H_DN = 4
DK = 128
DN_W = H_DN * DK
CONV_K = 5
CHUNK = 64
H_MLA = 8
Q_RANK = 384
KV_RANK = 256
NOPE = 64
ROPE = 32
V_DIM = 64
QK_SCALE = (NOPE + ROPE) ** -0.5
ROPE_BASE = 10000.0
F_GROUPS = 4
F_GDIM = 128
F_W = F_GROUPS * F_GDIM
D_FF = 2816
N_EXP = 8
TOP_K = 2
E_FF = 3584
EPS = 1e-6

MX = B * T
MC = B * CTX
M = MX + MC

HP = 128
C_GATES = 0
C_QKV = 3 * D
C_Z = C_QKV + 3 * DN_W
C_F = C_Z + DN_W
C_CQ = C_F + F_W
CQ_PAD = 512
C_CKV = C_CQ + CQ_PAD
N_MAIN = C_CKV + KV_RANK
N_SMALL = 256

V7X_VMEM_BYTES = 64 * 1024 * 1024
V7X_VMEM_USABLE = 56 * 1024 * 1024


def _cparams(sem, vmem_est_bytes):
    limit = int(min(V7X_VMEM_USABLE, max(32 * 1024 * 1024, 2 * vmem_est_bytes)))
    return pltpu.CompilerParams(dimension_semantics=sem, vmem_limit_bytes=limit)


def _norm_mod(x, gain, shift, scale):
    y = x * lax.rsqrt(jnp.mean(x * x, -1, keepdims=True) + EPS) * gain
    return y * (1.0 + scale) + shift


def _bdot(a, b):
    return jnp.dot(a, b, preferred_element_type=F32)


def _split3(v):
    hi = v.astype(BF16)
    r1 = v - hi.astype(F32)
    mid = r1.astype(BF16)
    lo = (r1 - mid.astype(F32)).astype(BF16)
    return hi, mid, lo


def _mod_kernel(a_ref, w_ref, b_ref, o_ref):
    a = a_ref[...]
    a = a * jax.nn.sigmoid(a)
    w = w_ref[...]
    a_hi = a.astype(BF16)
    a_lo = (a - a_hi.astype(F32)).astype(BF16)
    w_hi = w.astype(BF16)
    w_lo = (w - w_hi.astype(F32)).astype(BF16)
    o_ref[...] = _bdot(a_hi, w_hi) + _bdot(a_lo, w_hi) + _bdot(a_hi, w_lo) + b_ref[...]


def _mod_vectors(c, c_ctx, w_mod, b_mod):
    tn = 1536
    a = jnp.concatenate([c, c_ctx[None, :], jnp.zeros((3, D), F32)], 0)
    out = pl.pallas_call(
        _mod_kernel,
        grid=(DEPTH, 6 * D // tn),
        in_specs=[
            pl.BlockSpec((8, D), lambda l, j: (0, 0)),
            pl.BlockSpec((None, D, tn), lambda l, j: (l, 0, j)),
            pl.BlockSpec((None, 1, tn), lambda l, j: (l, 0, j)),
        ],
        out_specs=pl.BlockSpec((None, 8, tn), lambda l, j: (l, 0, j)),
        out_shape=jax.ShapeDtypeStruct((DEPTH, 8, 6 * D), F32),
        compiler_params=_cparams(("arbitrary", "arbitrary"), 2 * D * tn * 4 * 2),
        name="mod_vectors",
    )(a, w_mod, b_mod.reshape(DEPTH, 1, 6 * D))
    return out.reshape(DEPTH, 8, 6, D)[:, : B + 1]


def _mod_spec(tm):
    tpb = T // tm
    return pl.BlockSpec((None, 6, D), lambda i, *_: (jnp.minimum(i // tpb, B), 0, 0))


def _inproj_kernel(x_ref, g_ref, mod_ref, w_ref, ws_ref, o_ref, os_ref, h_ref):
    @pl.when(pl.program_id(1) == 0)
    def _():
        hb = _norm_mod(x_ref[...], g_ref[...], mod_ref[0:1, :], mod_ref[1:2, :]).astype(BF16)
        h_ref[...] = hb
        os_ref[...] = _bdot(hb, ws_ref[...])

    o_ref[...] = _bdot(h_ref[...], w_ref[...]).astype(o_ref.dtype)


def _inproj(X, gain, mod, w_main, w_small):
    tm, tn = 1024, 3200
    est = 2 * (tm * D * 4 + D * tn * 2 + D * N_SMALL * 2 + tm * tn * 2 + tm * N_SMALL * 4) + tm * D * 2
    return pl.pallas_call(
        _inproj_kernel,
        grid=(M // tm, N_MAIN // tn),
        in_specs=[
            pl.BlockSpec((tm, D), lambda i, j: (i, 0)),
            pl.BlockSpec((1, D), lambda i, j: (0, 0)),
            _mod_spec(tm),
            pl.BlockSpec((D, tn), lambda i, j: (0, j)),
            pl.BlockSpec((D, N_SMALL), lambda i, j: (0, 0)),
        ],
        out_specs=[
            pl.BlockSpec((tm, tn), lambda i, j: (i, j)),
            pl.BlockSpec((tm, N_SMALL), lambda i, j: (i, 0)),
        ],
        out_shape=[jax.ShapeDtypeStruct((M, N_MAIN), BF16), jax.ShapeDtypeStruct((M, N_SMALL), F32)],
        scratch_shapes=[pltpu.VMEM((tm, D), BF16)],
        compiler_params=_cparams(("arbitrary", "arbitrary"), est),
        name="in_proj",
    )(X, gain, mod, w_main, w_small)


DN_TT = 256
DN_HALO = 16


def _dnpre_kernel(cur_ref, prev_ref, next_ref, sm_ref, cw_ref, al_ref, dtb_ref, qkv_ref, gb_ref):
    i = pl.program_id(0)
    tiles_per_seq = T // DN_TT
    is_ctx = i >= MX // DN_TT
    first = jnp.logical_or(is_ctx, i % tiles_per_seq == 0)
    last = jnp.logical_or(is_ctx, i % tiles_per_seq == tiles_per_seq - 1)
    prev = jnp.where(first, 0.0, prev_ref[...].astype(F32))
    nxt = jnp.where(last, 0.0, next_ref[...].astype(F32))
    xc = jnp.concatenate([prev, cur_ref[...].astype(F32), nxt], axis=0)
    n = xc.shape[0]
    half = CONV_K // 2
    acc = xc * cw_ref[half:half + 1, :]
    for j in range(CONV_K):
        if j != half:
            acc = acc + pltpu.roll(xc, (half - j) % n, 0) * cw_ref[j:j + 1, :]
    y = acc[DN_HALO:DN_HALO + DN_TT]
    y = y * jax.nn.sigmoid(y)
    for p in range(3 * H_DN):
        blk = y[:, p * DK:(p + 1) * DK]
        if p < 2 * H_DN:
            blk = blk * lax.rsqrt(jnp.sum(blk * blk, -1, keepdims=True) + EPS)
        qkv_ref[:, p * DK:(p + 1) * DK] = blk.astype(BF16)

    sm = sm_ref[...]
    lane = lax.broadcasted_iota(jnp.int32, sm.shape, 1)
    g = -jnp.exp(al_ref[...]) * jax.nn.softplus(sm + dtb_ref[...])
    gb_ref[...] = jnp.where(lane < 2 * H_DN, g, jax.nn.sigmoid(sm))


def _dnpre(u, usm, conv_w, a_log, dt_bias):
    tt = DN_TT
    nb16 = tt // DN_HALO
    qkv_cb = C_QKV // (3 * DN_W)
    pad = jnp.zeros((1, 128 - 2 * H_DN), F32)
    al = jnp.concatenate([a_log.reshape(1, 2 * H_DN), pad], 1)
    dtb = jnp.concatenate([dt_bias.reshape(1, 2 * H_DN), pad], 1)
    est = 2 * (tt * 1536 * 2 * 2 + tt * 128 * 4 * 2) + 6 * (tt + 32) * 1536 * 4
    return pl.pallas_call(
        _dnpre_kernel,
        grid=(M // tt,),
        in_specs=[
            pl.BlockSpec((tt, 3 * DN_W), lambda i: (i, qkv_cb)),
            pl.BlockSpec((DN_HALO, 3 * DN_W), lambda i: (jnp.maximum(i * nb16 - 1, 0), qkv_cb)),
            pl.BlockSpec((DN_HALO, 3 * DN_W), lambda i: (jnp.minimum((i + 1) * nb16, M // DN_HALO - 1), qkv_cb)),
            pl.BlockSpec((tt, 128), lambda i: (i, 0)),
            pl.BlockSpec((CONV_K, 3 * DN_W), lambda i: (0, 0)),
            pl.BlockSpec((1, 128), lambda i: (0, 0)),
            pl.BlockSpec((1, 128), lambda i: (0, 0)),
        ],
        out_specs=[
            pl.BlockSpec((tt, 3 * DN_W), lambda i: (_dn_group(i), 0)),
            pl.BlockSpec((tt, 128), lambda i: (_dn_group(i), 0)),
        ],
        out_shape=[jax.ShapeDtypeStruct((M, 3 * DN_W), BF16), jax.ShapeDtypeStruct((M, 128), F32)],
        compiler_params=_cparams(("arbitrary",), est),
        name="dn_pre",
    )(u, u, u, usm, conv_w, al, dtb)


DN_GC = DN_TT // CHUNK
DN_XG = T // DN_TT
DN_NB = 4


def _dnscan_kernel(qkv0_ref, qkv1_ref, gb0_ref, gb1_ref, o0_ref, o1_ref, s_ref):
    @pl.when(pl.program_id(1) == 0)
    def _():
        s_ref[...] = jnp.zeros_like(s_ref)

    def body(j, carry):
        _dnscan_chunk(j, qkv0_ref, qkv1_ref, gb0_ref, gb1_ref, o0_ref, o1_ref, s_ref)
        return carry

    lax.fori_loop(0, DN_GC, body, 0)


def _dnscan_chunk(j, qkv0_ref, qkv1_ref, gb0_ref, gb1_ref, o0_ref, o1_ref, s_ref):
    C = CHUNK
    W2 = 2 * DK
    ri = lax.broadcasted_iota(jnp.int32, (C, 2 * C), 0)
    li = lax.broadcasted_iota(jnp.int32, (C, 2 * C), 1)
    lm = li % C
    first = li < C
    firstw = lax.broadcasted_iota(jnp.int32, (C, W2), 1) < DK
    eye2 = (ri == lm).astype(F32)
    ii = lax.broadcasted_iota(jnp.int32, (C, C), 0)
    jj = lax.broadcasted_iota(jnp.int32, (C, C), 1)
    sr = lax.broadcasted_iota(jnp.int32, (W2, W2), 0) < DK
    sc = lax.broadcasted_iota(jnp.int32, (W2, W2), 1) < DK
    sdiag = sr == sc
    qscale = DK ** -0.5
    nt = (((1,), (1,)), ((), ()))
    tn = (((0,), (0,)), ((), ()))

    def bd(m2):
        return jnp.concatenate([jnp.where(first, m2, 0.0), jnp.where(first, 0.0, m2)], 0).astype(BF16)

    def bdw(m2):
        return jnp.concatenate([jnp.where(firstw, m2, 0.0), jnp.where(firstw, 0.0, m2)], 0).astype(BF16)

    ch = []
    for bb, d in [(bb, d) for bb in range(DN_NB) for d in range(2)]:
        rows = pl.ds(pl.multiple_of(bb * DN_TT + (j if d == 0 else DN_GC - 1 - j) * C, C), C)
        qkv_ref = (qkv0_ref, qkv1_ref)[d]
        gbv = (gb0_ref, gb1_ref)[d][rows, :]
        incl = (ri >= lm) if d == 0 else (ri <= lm)
        strict = (ri > lm) if d == 0 else (ri < lm)
        tri = ((ii >= jj) if d == 0 else (ii <= jj)).astype(BF16)
        g_hi, g_mid, g_lo = _split3(gbv)
        gc = _bdot(tri, g_hi) + _bdot(tri, g_mid) + _bdot(tri, g_lo)
        gct = jnp.concatenate([gc, gc], 0).T
        last_row = C - 1 if d == 0 else 0
        for p in range(H_DN // 2):
            la = d * H_DN + 2 * p
            lb = la + 1
            col2 = jnp.where(first, gc[:, la:la + 1], gc[:, lb:lb + 1])
            row2 = jnp.where(first[0:1], gct[la:la + 1, :], gct[lb:lb + 1, :])
            colw = jnp.where(firstw, gc[:, la:la + 1], gc[:, lb:lb + 1])
            betaw = jnp.where(firstw, gbv[:, 2 * H_DN + la:2 * H_DN + la + 1], gbv[:, 2 * H_DN + lb:2 * H_DN + lb + 1])
            gl_a = gc[last_row:last_row + 1, la:la + 1]
            gl_b = gc[last_row:last_row + 1, lb:lb + 1]
            k2 = qkv_ref[rows, DN_W + p * W2:DN_W + (p + 1) * W2].astype(F32)
            kb = k2 * betaw
            qs = qkv_ref[rows, p * W2:(p + 1) * W2].astype(F32) * qscale
            ecolw = jnp.exp(colw)
            vbeta = qkv_ref[rows, 2 * DN_W + p * W2:2 * DN_W + (p + 1) * W2].astype(F32) * betaw
            kbe = kb * ecolw
            ch.append(dict(
                d=d, p=p, strict=strict, rows=rows, si=(bb * 2 + d) * (H_DN // 2) + p,
                dec=jnp.exp(jnp.where(incl, col2 - row2, -jnp.inf)),
                lhs=jnp.concatenate([kb.astype(BF16), qs.astype(BF16)], 0),
                kbd=bdw(k2),
                qg=(qs * ecolw).astype(BF16),
                kd=(k2 * jnp.exp(jnp.where(firstw, gl_a, gl_b) - colw)).astype(BF16),
                rhs=jnp.concatenate([
                    jnp.concatenate([vbeta[:, 0:DK], kbe[:, 0:DK]], 1),
                    jnp.concatenate([vbeta[:, DK:W2], kbe[:, DK:W2]], 1)], 0).astype(BF16),
                gls=jnp.where(sr, jnp.exp(gl_a), jnp.exp(gl_b)),
            ))

    for c in ch:
        c["kq"] = lax.dot_general(c["lhs"], c["kbd"], nt, preferred_element_type=F32)
    for c in ch:
        kq = c["kq"]
        nmat = -jnp.where(c["strict"], kq[0:C] * c["dec"], 0.0)
        c["attn"] = (kq[C:2 * C] * c["dec"]).astype(BF16)
        c["x"] = eye2 + nmat
        c["nm"] = nmat
    for c in ch:
        c["pw"] = _bdot(c["nm"].astype(BF16), bd(c["nm"]))
    for it in range(5):
        for c in ch:
            c["pbd"] = bd(c["pw"])
        for c in ch:
            c["x"] = c["x"] + _bdot(c["x"].astype(BF16), c["pbd"])
        if it < 4:
            for c in ch:
                c["pw"] = _bdot(c["pw"].astype(BF16), c["pbd"])
    for c in ch:
        x = c["x"]
        xl = jnp.concatenate([jnp.where(first, x, 0.0), jnp.where(first, 0.0, x)], 0).astype(BF16)
        uw = _bdot(xl, c["rhs"])
        c["u"] = jnp.concatenate([uw[0:C, 0:DK], uw[C:2 * C, 0:DK]], 1)
        w = jnp.concatenate([uw[0:C, DK:W2], uw[C:2 * C, DK:W2]], 1)
        c["wq"] = jnp.concatenate([w.astype(BF16), c["qg"]], 0)
    for c in ch:
        c["s"] = s_ref[c["si"]]
        c["wqs"] = _bdot(c["wq"], c["s"].astype(BF16))
    for c in ch:
        c["vn"] = c["u"] - c["wqs"][0:C]
    for c in ch:
        o = c["wqs"][C:2 * C] + _bdot(c["attn"], bdw(c["vn"]))
        o_ref = (o0_ref, o1_ref)[c["d"]]
        o_ref[c["rows"], c["p"] * W2:(c["p"] + 1) * W2] = o.astype(o_ref.dtype)
    for c in ch:
        kv = lax.dot_general(c["kd"], c["vn"].astype(BF16), tn, preferred_element_type=F32)
        s_ref[c["si"]] = c["s"] * c["gls"] + jnp.where(sdiag, kv, 0.0)


def _dn_group(i):
    b = i // DN_XG
    g = i % DN_XG
    return jnp.where(i < MX // DN_TT, ((b // DN_NB) * DN_XG + g) * DN_NB + b % DN_NB, i)


def _dnscan(qkv, gb):
    cblk = MX // (DN_NB * DN_TT)
    rows = DN_NB * DN_TT

    def idx0(bp, s):
        return jnp.where(s == 0, cblk + bp, bp * DN_XG + s - 1)

    def idx1(bp, s):
        return jnp.where(s == 0, cblk + bp, bp * DN_XG + DN_XG - s)

    n_state = DN_NB * 2 * (H_DN // 2)
    est = 2 * 2 * (rows * 1536 * 2 + rows * 128 * 4 + rows * 512 * 4) + n_state * 4 * DK * DK * 4 + (4 << 20)
    return pl.pallas_call(
        _dnscan_kernel,
        grid=(B // DN_NB, 1 + DN_XG),
        in_specs=[
            pl.BlockSpec((rows, 3 * DN_W), lambda bp, s: (idx0(bp, s), 0)),
            pl.BlockSpec((rows, 3 * DN_W), lambda bp, s: (idx1(bp, s), 0)),
            pl.BlockSpec((rows, 128), lambda bp, s: (idx0(bp, s), 0)),
            pl.BlockSpec((rows, 128), lambda bp, s: (idx1(bp, s), 0)),
        ],
        out_specs=[
            pl.BlockSpec((rows, DN_W), lambda bp, s: (idx0(bp, s), 0)),
            pl.BlockSpec((rows, DN_W), lambda bp, s: (idx1(bp, s), 0)),
        ],
        out_shape=[jax.ShapeDtypeStruct((M, DN_W), BF16), jax.ShapeDtypeStruct((M, DN_W), BF16)],
        scratch_shapes=[pltpu.VMEM((n_state, 2 * DK, 2 * DK), F32)],
        compiler_params=_cparams(("arbitrary", "arbitrary"), est),
        name="dn_scan",
    )(qkv, qkv, gb, gb)


def _mlaproj_kernel(cq_ref, ckv_ref, sm_ref, cos_ref, sin_ref, gq_ref, gkv_ref,
                    wq1_ref, wq2_ref, wk_ref, wv_ref, q_ref, k_ref, v_ref):
    cq = cq_ref[...].astype(F32)
    hq = (cq * lax.rsqrt(jnp.sum(cq * cq, -1, keepdims=True) * (1.0 / Q_RANK) + EPS) * gq_ref[...]).astype(BF16)
    ckv = ckv_ref[...].astype(F32)
    hkv = (ckv * lax.rsqrt(jnp.mean(ckv * ckv, -1, keepdims=True) + EPS) * gkv_ref[...]).astype(BF16)
    q1 = _bdot(hq, wq1_ref[...])
    q2 = _bdot(hq, wq2_ref[...])
    kn = _bdot(hkv, wk_ref[...])
    vv = _bdot(hkv, wv_ref[...])
    cosq = cos_ref[...]
    sin = sin_ref[...]
    lane = lax.broadcasted_iota(jnp.int32, cosq.shape, 1)
    cosk = jnp.where(lane >= NOPE, cosq, 0.0)
    kpe = sm_ref[:, 0:HP] * cosk + sm_ref[:, HP:2 * HP] * sin
    for h in range(H_MLA):
        sl = slice(h * HP, (h + 1) * HP)
        q_ref[:, sl] = ((q1[:, sl] * cosq + q2[:, sl] * sin) * QK_SCALE).astype(BF16)
        k_ref[:, sl] = (kn[:, sl] + kpe).astype(BF16)
        v_ref[:, sl] = jnp.where(lane == V_DIM, 1.0, vv[:, sl]).astype(BF16)


def _mlaproj(u, usm, cos_t, sin_t, gq, gkv, wq1, wq2, wk, wv):
    tm = 512
    wide = H_MLA * HP
    est = 2 * (tm * (CQ_PAD + KV_RANK) * 2 + tm * 256 * 4 + 2 * tm * 128 * 4 + 2 * CQ_PAD * wide * 2
               + 2 * KV_RANK * wide * 2 + 3 * tm * wide * 2) + 6 * tm * wide * 4
    full = lambda shape: pl.BlockSpec(shape, lambda i: (0, 0))
    return pl.pallas_call(
        _mlaproj_kernel,
        grid=(M // tm,),
        in_specs=[
            pl.BlockSpec((tm, CQ_PAD), lambda i: (i, C_CQ // CQ_PAD)),
            pl.BlockSpec((tm, KV_RANK), lambda i: (i, C_CKV // KV_RANK)),
            pl.BlockSpec((tm, N_SMALL), lambda i: (i, 0)),
            pl.BlockSpec((tm, HP), lambda i: (i, 0)),
            pl.BlockSpec((tm, HP), lambda i: (i, 0)),
            full((1, CQ_PAD)), full((1, KV_RANK)),
            full((CQ_PAD, wide)), full((CQ_PAD, wide)), full((KV_RANK, wide)), full((KV_RANK, wide)),
        ],
        out_specs=[pl.BlockSpec((tm, wide), lambda i: (i, 0))] * 3,
        out_shape=[jax.ShapeDtypeStruct((M, wide), BF16)] * 3,
        compiler_params=_cparams(("arbitrary",), est),
        name="mla_proj",
    )(u, u, usm, cos_t, sin_t, gq, gkv, wq1, wq2, wk, wv)


ATT_TQ = 256
ATT_TK = 512
ATT_HG = 2


def _attn_kernel(q_ref, kx_ref, vx_ref, kc_ref, vc_ref, o_ref):
    qi = pl.program_id(2)
    nt = (((1,), (1,)), ((), ()))
    heads = [slice(g * HP, (g + 1) * HP) for g in range(ATT_HG)]
    qs = [q_ref[:, sl] for sl in heads]

    def scores(g, k_ref, rows):
        return lax.dot_general(qs[g], k_ref[rows, heads[g]], nt, preferred_element_type=F32)

    def update(carry, s, v_blk):
        m, acc = carry
        m_new = jnp.maximum(m, jnp.max(s, -1, keepdims=True))
        p = jnp.exp((s - m_new).astype(BF16))
        return m_new, jnp.exp(m - m_new) * acc + _bdot(p, v_blk)

    def finish(carries):
        for g, (_, acc) in enumerate(carries):
            o_ref[:, heads[g]] = (acc / acc[:, V_DIM:V_DIM + 1]).astype(o_ref.dtype)

    init = (jnp.full((ATT_TQ, 1), -jnp.inf, F32), jnp.zeros((ATT_TQ, HP), F32))
    everything = slice(None)

    def ctx_block():
        return [update(init, scores(g, kc_ref, everything), vc_ref[:, heads[g]]) for g in range(ATT_HG)]

    @pl.when(qi < T // ATT_TQ)
    def _():
        nblk = T // ATT_TK
        s_next = [scores(g, kx_ref, slice(0, ATT_TK)) for g in range(ATT_HG)]
        carries = ctx_block()
        for j in range(nblk):
            s = s_next
            if j + 1 < nblk:
                s_next = [scores(g, kx_ref, slice((j + 1) * ATT_TK, (j + 2) * ATT_TK)) for g in range(ATT_HG)]
            rows = slice(j * ATT_TK, (j + 1) * ATT_TK)
            carries = [update(carries[g], s[g], vx_ref[rows, heads[g]]) for g in range(ATT_HG)]
        finish(carries)

    @pl.when(qi >= T // ATT_TQ)
    def _():
        finish(ctx_block())


def _attention(q, k, v, with_ctx_queries):
    tq = ATT_TQ
    nqx = T // tq
    nq = nqx + (1 if with_ctx_queries else 0)
    cblk = MX // CTX
    rows = M if with_ctx_queries else MX

    def qidx(b, h, i):
        return (jnp.where(i < nqx, b * nqx + i, MX // tq + b), h)

    wide = ATT_HG * HP
    est = 2 * (2 * T * wide * 2 + 2 * CTX * wide * 2 + 2 * tq * wide * 2) + 8 * ATT_HG * tq * ATT_TK * 4
    return pl.pallas_call(
        _attn_kernel,
        grid=(B, H_MLA // ATT_HG, nq),
        in_specs=[
            pl.BlockSpec((tq, wide), qidx),
            pl.BlockSpec((T, wide), lambda b, h, i: (b, h)),
            pl.BlockSpec((T, wide), lambda b, h, i: (b, h)),
            pl.BlockSpec((CTX, wide), lambda b, h, i: (cblk + b, h)),
            pl.BlockSpec((CTX, wide), lambda b, h, i: (cblk + b, h)),
        ],
        out_specs=pl.BlockSpec((tq, wide), qidx),
        out_shape=jax.ShapeDtypeStruct((rows, H_MLA * HP), BF16),
        compiler_params=_cparams(("arbitrary", "arbitrary", "arbitrary"), est),
        name="mla_attention",
    )(q, k, v, k, v)


def _mm1_kernel(a_ref, b_ref, o_ref):
    o_ref[...] = _bdot(a_ref[...], b_ref[...]).astype(o_ref.dtype)


def _fourier_channels(u, w_ab):
    tm = 512
    est = 2 * (tm * F_W * 2 + F_W * 2 * F_W * 2 + tm * 2 * F_W * 2)
    return pl.pallas_call(
        _mm1_kernel,
        grid=(M // tm,),
        in_specs=[pl.BlockSpec((tm, F_W), lambda i: (i, C_F // F_W)),
                  pl.BlockSpec((F_W, 2 * F_W), lambda i: (0, 0))],
        out_specs=pl.BlockSpec((tm, 2 * F_W), lambda i: (i, 0)),
        out_shape=jax.ShapeDtypeStruct((M, 2 * F_W), BF16),
        compiler_params=_cparams(("arbitrary",), est),
        name="fourier_channels",
    )(u, w_ab)


def _dft_kernel(cs_ref, ab_ref, o_ref, acc_ref, *, nk):
    kk = pl.program_id(2)

    @pl.when(kk == 0)
    def _():
        acc_ref[...] = jnp.zeros_like(acc_ref)

    acc_ref[...] += _bdot(cs_ref[...], ab_ref[...])

    @pl.when(kk == nk - 1)
    def _():
        o_ref[...] = acc_ref[...].astype(o_ref.dtype)


def _fourier_tokens(cs, ab, seq, row0, tm, tk):
    nk = 2 * seq // tk
    kpb = seq // tk
    est = 2 * (tm * tk * 2 + tk * F_W * 2 + tm * F_W * 2) + tm * F_W * 4
    return pl.pallas_call(
        functools.partial(_dft_kernel, nk=nk),
        grid=(B, seq // tm, nk),
        in_specs=[
            pl.BlockSpec((tm, tk), lambda b, i, kk: (i, kk)),
            pl.BlockSpec((tk, F_W), lambda b, i, kk: (row0 // tk + b * kpb + kk % kpb, kk // kpb)),
        ],
        out_specs=pl.BlockSpec((tm, F_W), lambda b, i, kk: (b * (seq // tm) + i, 0)),
        out_shape=jax.ShapeDtypeStruct((B * seq, F_W), BF16),
        scratch_shapes=[pltpu.VMEM((tm, F_W), F32)],
        compiler_params=_cparams(("arbitrary", "arbitrary", "arbitrary"), est),
        name="fourier_tokens",
    )(cs, ab)


def _merge_kernel(x_ref, mod_ref, o0_ref, o1_ref, z_ref, yb_ref, ycx_ref, ycc_ref, gt_ref, gdn_ref,
                  wa_ref, wb_ref, wc_ref, wo_ref, out_ref, *, x_tiles):
    yc = jnp.where(pl.program_id(0) < x_tiles, ycx_ref[...], ycc_ref[...])
    o = o0_ref[...].astype(F32) + o1_ref[...].astype(F32)
    z = z_ref[...].astype(F32)
    gdn = gdn_ref[...]
    ya = []
    for h in range(H_DN):
        sl = slice(h * DK, (h + 1) * DK)
        oh = o[:, sl]
        zh = z[:, sl]
        yh = oh * lax.rsqrt(jnp.mean(oh * oh, -1, keepdims=True) + EPS) * gdn
        ya.append((yh * (zh * jax.nn.sigmoid(zh))).astype(BF16))
    ya = jnp.concatenate(ya, -1)
    s = jax.nn.sigmoid(gt_ref[:, 0:D].astype(F32)) * _bdot(ya, wa_ref[...])
    s = s + jax.nn.sigmoid(gt_ref[:, D:2 * D].astype(F32)) * _bdot(yb_ref[...], wb_ref[...])
    s = s + jax.nn.sigmoid(gt_ref[:, 2 * D:3 * D].astype(F32)) * _bdot(yc, wc_ref[...])
    mix = _bdot(s.astype(BF16), wo_ref[...])
    out_ref[...] = x_ref[...] + mod_ref[2:3, :] * mix


def _merge(X, mod, o0, o1, u, yb, ycx, ycc, gdn, wa, wb, wc, wo, rows):
    tm = 256
    x_tiles = MX // tm
    full = lambda shape: pl.BlockSpec(shape, lambda i: (0, 0))
    est = 2 * (2 * tm * D * 4 + 2 * tm * DN_W * 4 + tm * DN_W * 2 * 3 + tm * D * 2 + tm * 3 * D * 2
               + (2 * DN_W + 2 * D) * D * 2) + 6 * tm * D * 4
    return pl.pallas_call(
        functools.partial(_merge_kernel, x_tiles=x_tiles),
        grid=(rows // tm,),
        in_specs=[
            pl.BlockSpec((tm, D), lambda i: (i, 0)),
            _mod_spec(tm),
            pl.BlockSpec((tm, DN_W), lambda i: (_dn_group(i), 0)),
            pl.BlockSpec((tm, DN_W), lambda i: (_dn_group(i), 0)),
            pl.BlockSpec((tm, DN_W), lambda i: (i, C_Z // DN_W)),
            pl.BlockSpec((tm, H_MLA * HP), lambda i: (i, 0)),
            pl.BlockSpec((tm, F_W), lambda i: (jnp.minimum(i, x_tiles - 1), 0)),
            pl.BlockSpec((tm, F_W), lambda i: (jnp.maximum(i - x_tiles, 0), 0)),
            pl.BlockSpec((tm, 3 * D), lambda i: (i, 0)),
            full((1, DK)),
            full((DN_W, D)), full((H_MLA * HP, D)), full((F_W, D)), full((D, D)),
        ],
        out_specs=pl.BlockSpec((tm, D), lambda i: (i, 0)),
        out_shape=jax.ShapeDtypeStruct((rows, D), F32),
        compiler_params=_cparams(("arbitrary",), est),
        name="merge",
    )(X, mod, o0, o1, u, yb, ycx, ycc, u, gdn, wa, wb, wc, wo)


def _ffn_kernel(x_ref, g_ref, mod_ref, wg_ref, wu_ref, wd_ref, o_ref, h_ref, acc_ref, *, nf):
    f = pl.program_id(1)

    @pl.when(f == 0)
    def _():
        h_ref[...] = _norm_mod(x_ref[...], g_ref[...], mod_ref[3:4, :], mod_ref[4:5, :]).astype(BF16)
        acc_ref[...] = jnp.zeros_like(acc_ref)

    h = h_ref[...]
    a = _bdot(h, wg_ref[...])
    up = _bdot(h, wu_ref[...])
    act = (a * jax.nn.sigmoid(a) * up).astype(BF16)
    acc_ref[...] += _bdot(act, wd_ref[...])

    @pl.when(f == nf - 1)
    def _():
        o_ref[...] = x_ref[...] + mod_ref[5:6, :] * acc_ref[...]


def _ffn(X, gain, mod, wg, wu, wd):
    tm, tf = 512, 1408
    nf = D_FF // tf
    est = 2 * (2 * tm * D * 4 + 3 * D * tf * 2) + tm * D * 6 + 3 * tm * tf * 4
    return pl.pallas_call(
        functools.partial(_ffn_kernel, nf=nf),
        grid=(M // tm, nf),
        in_specs=[
            pl.BlockSpec((tm, D), lambda i, f: (i, 0)),
            pl.BlockSpec((1, D), lambda i, f: (0, 0)),
            _mod_spec(tm),
            pl.BlockSpec((D, tf), lambda i, f: (0, f)),
            pl.BlockSpec((D, tf), lambda i, f: (0, f)),
            pl.BlockSpec((tf, D), lambda i, f: (f, 0)),
        ],
        out_specs=pl.BlockSpec((tm, D), lambda i, f: (i, 0)),
        out_shape=jax.ShapeDtypeStruct((M, D), F32),
        scratch_shapes=[pltpu.VMEM((tm, D), BF16), pltpu.VMEM((tm, D), F32)],
        compiler_params=_cparams(("arbitrary", "arbitrary"), est),
        name="ffn_dense",
    )(X, gain, mod, wg, wu, wd)


def _router_kernel(x_ref, g_ref, mod_ref, wr_ref, h_ref, r_ref):
    h = _norm_mod(x_ref[...], g_ref[...], mod_ref[3:4, :], mod_ref[4:5, :])
    h_ref[...] = h
    h_hi = h.astype(BF16)
    h_lo = (h - h_hi.astype(F32)).astype(BF16)
    wr = wr_ref[...]
    w_hi = wr.astype(BF16)
    w_lo = (wr - w_hi.astype(F32)).astype(BF16)
    logits = _bdot(h_hi, w_hi) + _bdot(h_lo, w_hi) + _bdot(h_hi, w_lo)
    lane = lax.broadcasted_iota(jnp.int32, logits.shape, 1)
    logits = jnp.where(lane < N_EXP, logits, -jnp.inf)
    m1 = jnp.max(logits, -1, keepdims=True)
    i1 = jnp.min(jnp.where(logits == m1, lane, 128), -1, keepdims=True)
    rest = jnp.where(lane == i1, -jnp.inf, logits)
    m2 = jnp.max(rest, -1, keepdims=True)
    i2 = jnp.min(jnp.where(rest == m2, lane, 128), -1, keepdims=True)
    e2 = jnp.exp(m2 - m1)
    w1 = 1.0 / (1.0 + e2)
    w2 = e2 / (1.0 + e2)
    r = jnp.where(lane == 0, i1.astype(F32), 0.0)
    r = jnp.where(lane == 1, i2.astype(F32), r)
    r = jnp.where(lane == 2, w1, r)
    r = jnp.where(lane == 3, w2, r)
    r_ref[...] = r


def _router(X, gain, mod, wr_pad):
    tm = 512
    est = 2 * (tm * D * 4 + D * 128 * 4 + tm * D * 2 + tm * 128 * 4) + 4 * tm * D * 4
    return pl.pallas_call(
        _router_kernel,
        grid=(MX // tm,),
        in_specs=[
            pl.BlockSpec((tm, D), lambda i: (i, 0)),
            pl.BlockSpec((1, D), lambda i: (0, 0)),
            _mod_spec(tm),
            pl.BlockSpec((D, 128), lambda i: (0, 0)),
        ],
        out_specs=[pl.BlockSpec((tm, D), lambda i: (i, 0)), pl.BlockSpec((tm, 128), lambda i: (i, 0))],
        out_shape=[jax.ShapeDtypeStruct((MX, D), F32), jax.ShapeDtypeStruct((MX, 128), F32)],
        compiler_params=_cparams(("arbitrary",), est),
        name="moe_router",
    )(X, gain, mod, wr_pad)


MOE_TM = 512
MOE_TF = 1792
MOE_ROWS = MX * TOP_K + N_EXP * MOE_TM
MOE_TILES = MOE_ROWS // MOE_TM


def _moe_kernel(te_ref, nt_ref, tok_ref, h_hbm, rw_ref, wg_ref, wu_ref, wd_ref, o_ref,
                hbuf, sem, hb_ref, acc_ref, *, nf):
    t = pl.program_id(0)
    f = pl.program_id(1)
    n_tiles = nt_ref[0]
    valid = t < n_tiles

    def row_copy(tile, r, slot):
        return pltpu.make_async_copy(
            h_hbm.at[pl.ds(tok_ref[tile * MOE_TM + r], 1)], hbuf.at[slot, pl.ds(r, 1)], sem.at[slot])

    def start_tile(tile, slot):
        def body(r, carry):
            row_copy(tile, r, slot).start()
            return carry
        lax.fori_loop(0, MOE_TM, body, 0, unroll=8)

    def wait_tile(tile, slot):
        def body(r, carry):
            row_copy(tile, r, slot).wait()
            return carry
        lax.fori_loop(0, MOE_TM, body, 0, unroll=8)

    @pl.when(jnp.logical_and(valid, jnp.logical_and(f == 0, t == 0)))
    def _():
        start_tile(0, 0)

    @pl.when(jnp.logical_and(valid, f == 0))
    def _():
        slot = t % 2

        @pl.when(t + 1 < n_tiles)
        def _():
            start_tile(t + 1, 1 - slot)

        wait_tile(t, slot)
        hb_ref[...] = hbuf[slot].astype(BF16)
        acc_ref[...] = jnp.zeros_like(acc_ref)

    @pl.when(valid)
    def _():
        h = hb_ref[...]
        a = _bdot(h, wg_ref[...])
        up = _bdot(h, wu_ref[...])
        act = (a * jax.nn.sigmoid(a) * up).astype(BF16)
        acc_ref[...] += _bdot(act, wd_ref[...])

    @pl.when(jnp.logical_and(valid, f == nf - 1))
    def _():
        o_ref[...] = (acc_ref[...] * rw_ref[...]).astype(o_ref.dtype)

    @pl.when(jnp.logical_and(jnp.logical_not(valid), f == nf - 1))
    def _():
        o_ref[...] = jnp.zeros_like(o_ref)


def _moe_experts(tile_expert, n_tiles, row_token, h, row_w, wg, wu, wd):
    tm, tf = MOE_TM, MOE_TF
    nf = E_FF // tf

    def fsel(t, f, nt):
        return jnp.where(t < nt[0], f, nf - 1)

    est = 2 * (tm * 128 * 4 + 3 * D * tf * 2 + tm * D * 2) + 2 * tm * D * 4 + tm * D * 6 + 3 * tm * tf * 4
    grid_spec = pltpu.PrefetchScalarGridSpec(
        num_scalar_prefetch=3,
        grid=(MOE_TILES, nf),
        in_specs=[
            pl.BlockSpec(memory_space=pl.ANY),
            pl.BlockSpec((tm, 1), lambda t, f, te, nt, tok: (t, 0)),
            pl.BlockSpec((None, D, tf), lambda t, f, te, nt, tok: (te[t], 0, fsel(t, f, nt))),
            pl.BlockSpec((None, D, tf), lambda t, f, te, nt, tok: (te[t], 0, fsel(t, f, nt))),
            pl.BlockSpec((None, tf, D), lambda t, f, te, nt, tok: (te[t], fsel(t, f, nt), 0)),
        ],
        out_specs=pl.BlockSpec((tm, D), lambda t, f, te, nt, tok: (t, 0)),
        scratch_shapes=[pltpu.VMEM((2, tm, D), F32), pltpu.SemaphoreType.DMA((2,)),
                        pltpu.VMEM((tm, D), BF16), pltpu.VMEM((tm, D), F32)],
    )
    return pl.pallas_call(
        functools.partial(_moe_kernel, nf=nf),
        grid_spec=grid_spec,
        out_shape=jax.ShapeDtypeStruct((MOE_ROWS, D), BF16),
        compiler_params=_cparams(("arbitrary", "arbitrary"), est),
        name="moe_experts",
    )(tile_expert, n_tiles, row_token, h, row_w, wg, wu, wd)


def _final_kernel(x_ref, mod_ref, y0_ref, y1_ref, g_ref, o_ref):
    x = x_ref[...] + mod_ref[5:6, :] * (y0_ref[...].astype(F32) + y1_ref[...].astype(F32))
    o_ref[...] = x * lax.rsqrt(jnp.mean(x * x, -1, keepdims=True) + EPS) * g_ref[...]


def _final(X, mod, y0, y1, gain):
    tm = 512
    est = 2 * (2 * tm * D * 4 + 2 * tm * D * 2) + 3 * tm * D * 4
    return pl.pallas_call(
        _final_kernel,
        grid=(MX // tm,),
        in_specs=[
            pl.BlockSpec((tm, D), lambda i: (i, 0)),
            _mod_spec(tm),
            pl.BlockSpec((tm, D), lambda i: (i, 0)),
            pl.BlockSpec((tm, D), lambda i: (i, 0)),
            pl.BlockSpec((1, D), lambda i: (0, 0)),
        ],
        out_specs=pl.BlockSpec((tm, D), lambda i: (i, 0)),
        out_shape=jax.ShapeDtypeStruct((MX, D), F32),
        compiler_params=_cparams(("arbitrary",), est),
        name="final_norm",
    )(X, mod, y0, y1, gain)


def _moe_routing(route):
    tm = MOE_TM
    e_flat = route[:, 0:TOP_K].astype(jnp.int32).reshape(-1)
    w_flat = route[:, TOP_K:2 * TOP_K].reshape(-1)
    n = e_flat.shape[0]
    eids = jnp.arange(N_EXP, dtype=jnp.int32)[None, :]

    def pick(onehot, table):
        return jnp.sum(onehot * table[None, :], 1)

    order = jnp.argsort(e_flat, stable=True).astype(jnp.int32)
    inv_order = jnp.argsort(order).astype(jnp.int32)
    oh_f = (e_flat[:, None] == eids).astype(jnp.int32)
    counts = jnp.sum(oh_f, 0)
    padded = ((counts + tm - 1) // tm) * tm
    start = jnp.cumsum(counts) - counts
    pend = jnp.cumsum(padded)
    pstart = pend - padded
    slot_row = pick(oh_f, pstart - start) + inv_order

    n_tiles = (pend[-1] // tm).astype(jnp.int32)
    tile_start = jnp.arange(MOE_TILES, dtype=jnp.int32) * tm
    tile_expert = jnp.minimum(jnp.sum(tile_start[:, None] >= pend[None, :], 1), N_EXP - 1).astype(jnp.int32)
    rows = jnp.arange(MOE_ROWS, dtype=jnp.int32)
    oh_r = (jnp.repeat(tile_expert, tm)[:, None] == eids).astype(jnp.int32)
    rank_r = rows - pick(oh_r, pstart)
    valid = jnp.logical_and(rank_r < pick(oh_r, counts), rows < pend[-1])
    slot_r = order[jnp.clip(pick(oh_r, start) + rank_r, 0, n - 1)]
    row_token = jnp.where(valid, slot_r // TOP_K, 0)
    row_w = jnp.where(valid, w_flat[slot_r], 0.0)
    last_expert = jnp.sum(jnp.where(jnp.arange(MOE_TILES) == n_tiles - 1, tile_expert, 0))
    tile_expert = jnp.where(tile_start < pend[-1], tile_expert, last_expert)
    return row_token, row_w.reshape(MOE_ROWS, 1), slot_row.reshape(MX, TOP_K), tile_expert, n_tiles.reshape(1)


def _dft_table(n):
    split = 64 if n > 64 else 1
    t = jnp.arange(n, dtype=jnp.int32)[None, :]
    a = jnp.arange(n // split, dtype=jnp.int32)[:, None]
    b = jnp.arange(split, dtype=jnp.int32)[:, None]
    na = n // split
    ang_a = ((a * t) % na).astype(F32) * (2.0 * np.pi / na)
    ang_b = ((b * t) % n).astype(F32) * (2.0 * np.pi / n)
    scale = n ** -0.5
    ca, sa = jnp.cos(ang_a)[:, None, :], jnp.sin(ang_a)[:, None, :]
    cb, sb = (jnp.cos(ang_b) * scale)[None, :, :], (jnp.sin(ang_b) * scale)[None, :, :]
    c = (ca * cb - sa * sb).reshape(n, n)
    s = (sa * cb + ca * sb).reshape(n, n)
    return jnp.concatenate([c, s], 1).astype(BF16)


def _channel_table():
    k = jnp.arange(F_GDIM, dtype=jnp.int32)
    ang = ((k[:, None] * k[None, :]) % F_GDIM).astype(F32) * (2.0 * np.pi / F_GDIM)
    scale = F_GDIM ** -0.5
    eye = jnp.eye(F_GROUPS, dtype=F32)
    return jnp.concatenate([jnp.kron(eye, jnp.cos(ang) * scale), jnp.kron(eye, -jnp.sin(ang) * scale)], 1).astype(BF16)


def _rope_tables():
    t = jnp.arange(T, dtype=jnp.int32)
    r = (t // GRID_W).astype(F32)[:, None]
    c = (t % GRID_W).astype(F32)[:, None]
    axis_dim = ROPE // 2
    inv = ROPE_BASE ** (-jnp.arange(0, axis_dim, 2, dtype=F32) / axis_dim)[None, :]
    cr, sr, cc, sc = jnp.cos(r * inv), jnp.sin(r * inv), jnp.cos(c * inv), jnp.sin(c * inv)
    cos32 = jnp.concatenate([cr, cr, cc, cc], 1)
    sin32 = jnp.concatenate([-sr, sr, -sc, sc], 1)
    ones = jnp.ones((T, NOPE), F32)
    zeros = jnp.zeros((T, NOPE), F32)
    tail = jnp.zeros((T, HP - NOPE - ROPE), F32)
    cos_x = jnp.concatenate([ones, cos32, tail], 1)
    sin_x = jnp.concatenate([zeros, sin32, tail], 1)
    cos_c = jnp.concatenate([jnp.ones((MC, NOPE + ROPE), F32), jnp.zeros((MC, HP - NOPE - ROPE), F32)], 1)
    cos_t = jnp.concatenate([jnp.tile(cos_x, (B, 1)), cos_c], 0)
    sin_t = jnp.concatenate([jnp.tile(sin_x, (B, 1)), jnp.zeros((MC, HP), F32)], 0)
    return cos_t, sin_t


_ROPE_SWAP = np.concatenate([np.arange(8, 16), np.arange(0, 8), np.arange(24, 32), np.arange(16, 24)])


def _layer_weights(w_in, w_qup, w_kvup, w_branch):
    sizes = (DN_W, DN_W, DN_W, DN_W, 2 * H_DN, 2 * H_DN, Q_RANK, KV_RANK, ROPE, F_W, 3 * D)
    offs = np.concatenate([[0], np.cumsum(sizes)])
    seg = [w_in[:, offs[i]:offs[i + 1]] for i in range(len(sizes))]
    dq, dk, dv, dz, da, db, cq, ckv, kpe, fo, gates = seg
    zc = lambda n: jnp.zeros((D, n), F32)
    w_main = jnp.concatenate([gates, dq, dk, dv, dz, fo, cq, zc(CQ_PAD - Q_RANK), ckv], 1).astype(BF16)
    w_small = jnp.concatenate([da, db, zc(NOPE - 4 * H_DN), kpe, zc(HP - NOPE - ROPE),
                               zc(NOPE), kpe[:, _ROPE_SWAP], zc(HP - NOPE - ROPE)], 1).astype(BF16)

    qh = w_qup.reshape(Q_RANK, H_MLA, NOPE + ROPE)
    zq = lambda n: jnp.zeros((Q_RANK, H_MLA, n), F32)
    wq1 = jnp.concatenate([qh, zq(HP - NOPE - ROPE)], 2).reshape(Q_RANK, H_MLA * HP)
    wq2 = jnp.concatenate([zq(NOPE), qh[:, :, NOPE:][:, :, _ROPE_SWAP], zq(HP - NOPE - ROPE)], 2).reshape(Q_RANK, H_MLA * HP)
    rowpad = jnp.zeros((CQ_PAD - Q_RANK, H_MLA * HP), F32)
    wq1 = jnp.concatenate([wq1, rowpad], 0).astype(BF16)
    wq2 = jnp.concatenate([wq2, rowpad], 0).astype(BF16)
    kvh = w_kvup.reshape(KV_RANK, H_MLA, NOPE + V_DIM)
    zk = jnp.zeros((KV_RANK, H_MLA, HP - NOPE), F32)
    wk = jnp.concatenate([kvh[:, :, :NOPE], zk], 2).reshape(KV_RANK, H_MLA * HP).astype(BF16)
    wv = jnp.concatenate([kvh[:, :, NOPE:], zk], 2).reshape(KV_RANK, H_MLA * HP).astype(BF16)

    wb = w_branch[1].reshape(H_MLA, V_DIM, D)
    wb = jnp.concatenate([wb, jnp.zeros((H_MLA, HP - V_DIM, D), F32)], 1).reshape(H_MLA * HP, D)
    return dict(w_main=w_main, w_small=w_small, wq1=wq1, wq2=wq2, wk=wk, wv=wv,
                wa=w_branch[0].astype(BF16), wb=wb.astype(BF16), wc=w_branch[2].astype(BF16))


def kernel(x, c, ctx, c_ctx, w_mod, b_mod, norm_mix, norm_ffn, w_in, dn_conv, dn_a_log, dn_dt_bias, dn_norm, mla_q_norm, mla_kv_norm, mla_w_qup, mla_w_kvup, w_branch, w_out, ffn_w_gate, ffn_w_up, ffn_w_down, moe_router, moe_w_gate, moe_w_up, moe_w_down, final_norm):
    X = jnp.concatenate([x.reshape(MX, D), ctx.reshape(MC, D)], 0)
    mods = _mod_vectors(c, c_ctx, w_mod, b_mod)
    cos_t, sin_t = _rope_tables()
    cs_x = _dft_table(T)
    cs_c = _dft_table(CTX)
    w_ab = _channel_table()

    out = None
    for l in range(DEPTH):
        last = l == DEPTH - 1
        mod = mods[l]
        lw = _layer_weights(w_in[l], mla_w_qup[l], mla_w_kvup[l], w_branch[l])
        u, usm = _inproj(X, norm_mix[l].reshape(1, D), mod, lw["w_main"], lw["w_small"])
        qkv, gb = _dnpre(u, usm, dn_conv[l], dn_a_log[l], dn_dt_bias[l])
        o0, o1 = _dnscan(qkv, gb)
        gq = jnp.concatenate([mla_q_norm[l], jnp.zeros((CQ_PAD - Q_RANK,), F32)]).reshape(1, CQ_PAD)
        q, k, v = _mlaproj(u, usm, cos_t, sin_t, gq, mla_kv_norm[l].reshape(1, KV_RANK),
                           lw["wq1"], lw["wq2"], lw["wk"], lw["wv"])
        yb = _attention(q, k, v, with_ctx_queries=not last)
        ab = _fourier_channels(u, w_ab)
        ycx = _fourier_tokens(cs_x, ab, T, 0, 1024, 2048)
        ycc = ycx if last else _fourier_tokens(cs_c, ab, CTX, MX, CTX, CTX)
        rows = MX if last else M
        X = _merge(X, mod, o0, o1, u, yb, ycx, ycc, dn_norm[l].reshape(1, DK),
                   lw["wa"], lw["wb"], lw["wc"], w_out[l].astype(BF16), rows)
        e = l // 2
        if l % 2 == 0:
            X = _ffn(X, norm_ffn[l].reshape(1, D), mod, ffn_w_gate[e].astype(BF16),
                     ffn_w_up[e].astype(BF16), ffn_w_down[e].astype(BF16))
        else:
            wr = jnp.concatenate([moe_router[e], jnp.zeros((D, 128 - N_EXP), F32)], 1)
            h, route = _router(X, norm_ffn[l].reshape(1, D), mod, wr)
            row_token, row_w, slot_row, tile_expert, n_tiles = _moe_routing(route)
            y = _moe_experts(tile_expert, n_tiles, row_token, h, row_w, moe_w_gate[e].astype(BF16),
                             moe_w_up[e].astype(BF16), moe_w_down[e].astype(BF16))
            y0 = jnp.take(y, slot_row[:, 0], axis=0)
            y1 = jnp.take(y, slot_row[:, 1], axis=0)
            out = _final(X, mod, y0, y1, final_norm.reshape(1, D))
    return out.reshape(B, T, D)
```

```python
import functools

import numpy as np
import jax
import jax.numpy as jnp
from jax import lax
from jax.experimental import pallas as pl
from jax.experimental.pallas import tpu as pltpu

F32 = jnp.float32
BF16 = jnp.bfloat16

D = 1024
B = 4
T = 4096
CTX = 256
DEPTH = 2
GRID_W = 64
H_DN = 4
DK = 128
DN_W = H_DN * DK
CONV_K = 5
CHUNK = 64
H_MLA = 8
Q_RANK = 384
KV_RANK = 256
NOPE = 64
ROPE = 32
V_DIM = 64
QK_SCALE = (NOPE + ROPE) ** -0.5
ROPE_BASE = 10000.0
F_GROUPS = 4
F_GDIM = 128
F_W = F_GROUPS * F_GDIM
D_FF = 2816
N_EXP = 8
TOP_K = 2
E_FF = 3584
EPS = 1e-6

MX = B * T
MC = B * CTX
M = MX + MC

HP = 128
C_GATES = 0
C_QKV = 3 * D
C_Z = C_QKV + 3 * DN_W
C_F = C_Z + DN_W
C_CQ = C_F + F_W
CQ_PAD = 512
C_CKV = C_CQ + CQ_PAD
N_MAIN = C_CKV + KV_RANK
N_SMALL = 256

V7X_VMEM_BYTES = 64 * 1024 * 1024
V7X_VMEM_USABLE = 56 * 1024 * 1024


def _cparams(sem, vmem_est_bytes):
    limit = int(min(V7X_VMEM_USABLE, max(32 * 1024 * 1024, 2 * vmem_est_bytes)))
    return pltpu.CompilerParams(dimension_semantics=sem, vmem_limit_bytes=limit)


def _norm_mod(x, gain, shift, scale):
    y = x * lax.rsqrt(jnp.mean(x * x, -1, keepdims=True) + EPS) * gain
    return y * (1.0 + scale) + shift


def _bdot(a, b):
    return jnp.dot(a, b, preferred_element_type=F32)


def _split3(v):
    hi = v.astype(BF16)
    r1 = v - hi.astype(F32)
    mid = r1.astype(BF16)
    lo = (r1 - mid.astype(F32)).astype(BF16)
    return hi, mid, lo


def _mod_kernel(a_ref, w_ref, b_ref, o_ref):
    a = a_ref[...]
    a = a * jax.nn.sigmoid(a)
    w = w_ref[...]
    a_hi = a.astype(BF16)
    a_lo = (a - a_hi.astype(F32)).astype(BF16)
    w_hi = w.astype(BF16)
    w_lo = (w - w_hi.astype(F32)).astype(BF16)
    o_ref[...] = _bdot(a_hi, w_hi) + _bdot(a_lo, w_hi) + _bdot(a_hi, w_lo) + b_ref[...]


def _mod_vectors(c, c_ctx, w_mod, b_mod):
    tn = 1536
    a = jnp.concatenate([c, c_ctx[None, :], jnp.zeros((3, D), F32)], 0)
    out = pl.pallas_call(
        _mod_kernel,
        grid=(DEPTH, 6 * D // tn),
        in_specs=[
            pl.BlockSpec((8, D), lambda l, j: (0, 0)),
            pl.BlockSpec((None, D, tn), lambda l, j: (l, 0, j)),
            pl.BlockSpec((None, 1, tn), lambda l, j: (l, 0, j)),
        ],
        out_specs=pl.BlockSpec((None, 8, tn), lambda l, j: (l, 0, j)),
        out_shape=jax.ShapeDtypeStruct((DEPTH, 8, 6 * D), F32),
        compiler_params=_cparams(("arbitrary", "arbitrary"), 2 * D * tn * 4 * 2),
        name="mod_vectors",
    )(a, w_mod, b_mod.reshape(DEPTH, 1, 6 * D))
    return out.reshape(DEPTH, 8, 6, D)[:, : B + 1]


def _mod_spec(tm):
    tpb = T // tm
    return pl.BlockSpec((None, 6, D), lambda i, *_: (jnp.minimum(i // tpb, B), 0, 0))


def _inproj_kernel(x_ref, g_ref, mod_ref, w_ref, ws_ref, o_ref, os_ref, h_ref):
    @pl.when(pl.program_id(1) == 0)
    def _():
        hb = _norm_mod(x_ref[...], g_ref[...], mod_ref[0:1, :], mod_ref[1:2, :]).astype(BF16)
        h_ref[...] = hb
        os_ref[...] = _bdot(hb, ws_ref[...])

    o_ref[...] = _bdot(h_ref[...], w_ref[...]).astype(o_ref.dtype)


def _inproj(X, gain, mod, w_main, w_small):
    tm, tn = 1024, 3200
    est = 2 * (tm * D * 4 + D * tn * 2 + D * N_SMALL * 2 + tm * tn * 2 + tm * N_SMALL * 4) + tm * D * 2
    return pl.pallas_call(
        _inproj_kernel,
        grid=(M // tm, N_MAIN // tn),
        in_specs=[
            pl.BlockSpec((tm, D), lambda i, j: (i, 0)),
            pl.BlockSpec((1, D), lambda i, j: (0, 0)),
            _mod_spec(tm),
            pl.BlockSpec((D, tn), lambda i, j: (0, j)),
            pl.BlockSpec((D, N_SMALL), lambda i, j: (0, 0)),
        ],
        out_specs=[
            pl.BlockSpec((tm, tn), lambda i, j: (i, j)),
            pl.BlockSpec((tm, N_SMALL), lambda i, j: (i, 0)),
        ],
        out_shape=[jax.ShapeDtypeStruct((M, N_MAIN), BF16), jax.ShapeDtypeStruct((M, N_SMALL), F32)],
        scratch_shapes=[pltpu.VMEM((tm, D), BF16)],
        compiler_params=_cparams(("arbitrary", "arbitrary"), est),
        name="in_proj",
    )(X, gain, mod, w_main, w_small)


DN_TT = 256
DN_HALO = 16


def _dnpre_kernel(cur_ref, prev_ref, next_ref, sm_ref, cw_ref, al_ref, dtb_ref, qkv_ref, gb_ref):
    i = pl.program_id(0)
    tiles_per_seq = T // DN_TT
    is_ctx = i >= MX // DN_TT
    first = jnp.logical_or(is_ctx, i % tiles_per_seq == 0)
    last = jnp.logical_or(is_ctx, i % tiles_per_seq == tiles_per_seq - 1)
    prev = jnp.where(first, 0.0, prev_ref[...].astype(F32))
    nxt = jnp.where(last, 0.0, next_ref[...].astype(F32))
    xc = jnp.concatenate([prev, cur_ref[...].astype(F32), nxt], axis=0)
    n = xc.shape[0]
    half = CONV_K // 2
    acc = xc * cw_ref[half:half + 1, :]
    for j in range(CONV_K):
        if j != half:
            acc = acc + pltpu.roll(xc, (half - j) % n, 0) * cw_ref[j:j + 1, :]
    y = acc[DN_HALO:DN_HALO + DN_TT]
    y = y * jax.nn.sigmoid(y)
    for p in range(3 * H_DN):
        blk = y[:, p * DK:(p + 1) * DK]
        if p < 2 * H_DN:
            blk = blk * lax.rsqrt(jnp.sum(blk * blk, -1, keepdims=True) + EPS)
        qkv_ref[:, p * DK:(p + 1) * DK] = blk.astype(BF16)

    sm = sm_ref[...]
    lane = lax.broadcasted_iota(jnp.int32, sm.shape, 1)
    g = -jnp.exp(al_ref[...]) * jax.nn.softplus(sm + dtb_ref[...])
    gb_ref[...] = jnp.where(lane < 2 * H_DN, g, jax.nn.sigmoid(sm))


def _dnpre(u, usm, conv_w, a_log, dt_bias):
    tt = DN_TT
    nb16 = tt // DN_HALO
    qkv_cb = C_QKV // (3 * DN_W)
    pad = jnp.zeros((1, 128 - 2 * H_DN), F32)
    al = jnp.concatenate([a_log.reshape(1, 2 * H_DN), pad], 1)
    dtb = jnp.concatenate([dt_bias.reshape(1, 2 * H_DN), pad], 1)
    est = 2 * (tt * 1536 * 2 * 2 + tt * 128 * 4 * 2) + 6 * (tt + 32) * 1536 * 4
    return pl.pallas_call(
        _dnpre_kernel,
        grid=(M // tt,),
        in_specs=[
            pl.BlockSpec((tt, 3 * DN_W), lambda i: (i, qkv_cb)),
            pl.BlockSpec((DN_HALO, 3 * DN_W), lambda i: (jnp.maximum(i * nb16 - 1, 0), qkv_cb)),
            pl.BlockSpec((DN_HALO, 3 * DN_W), lambda i: (jnp.minimum((i + 1) * nb16, M // DN_HALO - 1), qkv_cb)),
            pl.BlockSpec((tt, 128), lambda i: (i, 0)),
            pl.BlockSpec((CONV_K, 3 * DN_W), lambda i: (0, 0)),
            pl.BlockSpec((1, 128), lambda i: (0, 0)),
            pl.BlockSpec((1, 128), lambda i: (0, 0)),
        ],
        out_specs=[
            pl.BlockSpec((tt, 3 * DN_W), lambda i: (_dn_group(i), 0)),
            pl.BlockSpec((tt, 128), lambda i: (_dn_group(i), 0)),
        ],
        out_shape=[jax.ShapeDtypeStruct((M, 3 * DN_W), BF16), jax.ShapeDtypeStruct((M, 128), F32)],
        compiler_params=_cparams(("arbitrary",), est),
        name="dn_pre",
    )(u, u, u, usm, conv_w, al, dtb)


DN_GC = DN_TT // CHUNK
DN_XG = T // DN_TT
DN_NB = 4


def _dnscan_kernel(qkv0_ref, qkv1_ref, gb0_ref, gb1_ref, o0_ref, o1_ref, s_ref):
    @pl.when(pl.program_id(1) == 0)
    def _():
        s_ref[...] = jnp.zeros_like(s_ref)

    def body(j, carry):
        _dnscan_chunk(j, qkv0_ref, qkv1_ref, gb0_ref, gb1_ref, o0_ref, o1_ref, s_ref)
        return carry

    lax.fori_loop(0, DN_GC, body, 0)


def _dnscan_chunk(j, qkv0_ref, qkv1_ref, gb0_ref, gb1_ref, o0_ref, o1_ref, s_ref):
    C = CHUNK
    W2 = 2 * DK
    ri = lax.broadcasted_iota(jnp.int32, (C, 2 * C), 0)
    li = lax.broadcasted_iota(jnp.int32, (C, 2 * C), 1)
    lm = li % C
    first = li < C
    firstw = lax.broadcasted_iota(jnp.int32, (C, W2), 1) < DK
    eye2 = (ri == lm).astype(F32)
    ii = lax.broadcasted_iota(jnp.int32, (C, C), 0)
    jj = lax.broadcasted_iota(jnp.int32, (C, C), 1)
    sr = lax.broadcasted_iota(jnp.int32, (W2, W2), 0) < DK
    sc = lax.broadcasted_iota(jnp.int32, (W2, W2), 1) < DK
    sdiag = sr == sc
    qscale = DK ** -0.5
    nt = (((1,), (1,)), ((), ()))
    tn = (((0,), (0,)), ((), ()))

    def bd(m2):
        return jnp.concatenate([jnp.where(first, m2, 0.0), jnp.where(first, 0.0, m2)], 0).astype(BF16)

    def bdw(m2):
        return jnp.concatenate([jnp.where(firstw, m2, 0.0), jnp.where(firstw, 0.0, m2)], 0).astype(BF16)

    ch = []
    for bb, d in [(bb, d) for bb in range(DN_NB) for d in range(2)]:
        rows = pl.ds(pl.multiple_of(bb * DN_TT + (j if d == 0 else DN_GC - 1 - j) * C, C), C)
        qkv_ref = (qkv0_ref, qkv1_ref)[d]
        gbv = (gb0_ref, gb1_ref)[d][rows, :]
        incl = (ri >= lm) if d == 0 else (ri <= lm)
        strict = (ri > lm) if d == 0 else (ri < lm)
        tri = ((ii >= jj) if d == 0 else (ii <= jj)).astype(BF16)
        g_hi, g_mid, g_lo = _split3(gbv)
        gc = _bdot(tri, g_hi) + _bdot(tri, g_mid) + _bdot(tri, g_lo)
        gct = jnp.concatenate([gc, gc], 0).T
        last_row = C - 1 if d == 0 else 0
        for p in range(H_DN // 2):
            la = d * H_DN + 2 * p
            lb = la + 1
            col2 = jnp.where(first, gc[:, la:la + 1], gc[:, lb:lb + 1])
            row2 = jnp.where(first[0:1], gct[la:la + 1, :], gct[lb:lb + 1, :])
            colw = jnp.where(firstw, gc[:, la:la + 1], gc[:, lb:lb + 1])
            betaw = jnp.where(firstw, gbv[:, 2 * H_DN + la:2 * H_DN + la + 1], gbv[:, 2 * H_DN + lb:2 * H_DN + lb + 1])
            gl_a = gc[last_row:last_row + 1, la:la + 1]
            gl_b = gc[last_row:last_row + 1, lb:lb + 1]
            k2 = qkv_ref[rows, DN_W + p * W2:DN_W + (p + 1) * W2].astype(F32)
            kb = k2 * betaw
            qs = qkv_ref[rows, p * W2:(p + 1) * W2].astype(F32) * qscale
            ecolw = jnp.exp(colw)
            vbeta = qkv_ref[rows, 2 * DN_W + p * W2:2 * DN_W + (p + 1) * W2].astype(F32) * betaw
            kbe = kb * ecolw
            ch.append(dict(
                d=d, p=p, strict=strict, rows=rows, si=(bb * 2 + d) * (H_DN // 2) + p,
                dec=jnp.exp(jnp.where(incl, col2 - row2, -jnp.inf)),
                lhs=jnp.concatenate([kb.astype(BF16), qs.astype(BF16)], 0),
                kbd=bdw(k2),
                qg=(qs * ecolw).astype(BF16),
                kd=(k2 * jnp.exp(jnp.where(firstw, gl_a, gl_b) - colw)).astype(BF16),
                rhs=jnp.concatenate([
                    jnp.concatenate([vbeta[:, 0:DK], kbe[:, 0:DK]], 1),
                    jnp.concatenate([vbeta[:, DK:W2], kbe[:, DK:W2]], 1)], 0).astype(BF16),
                gls=jnp.where(sr, jnp.exp(gl_a), jnp.exp(gl_b)),
            ))

    for c in ch:
        c["kq"] = lax.dot_general(c["lhs"], c["kbd"], nt, preferred_element_type=F32)
    for c in ch:
        kq = c["kq"]
        nmat = -jnp.where(c["strict"], kq[0:C] * c["dec"], 0.0)
        c["attn"] = (kq[C:2 * C] * c["dec"]).astype(BF16)
        c["x"] = eye2 + nmat
        c["nm"] = nmat
    for c in ch:
        c["pw"] = _bdot(c["nm"].astype(BF16), bd(c["nm"]))
    for it in range(5):
        for c in ch:
            c["pbd"] = bd(c["pw"])
        for c in ch:
            c["x"] = c["x"] + _bdot(c["x"].astype(BF16), c["pbd"])
        if it < 4:
            for c in ch:
                c["pw"] = _bdot(c["pw"].astype(BF16), c["pbd"])
    for c in ch:
        x = c["x"]
        xl = jnp.concatenate([jnp.where(first, x, 0.0), jnp.where(first, 0.0, x)], 0).astype(BF16)
        uw = _bdot(xl, c["rhs"])
        c["u"] = jnp.concatenate([uw[0:C, 0:DK], uw[C:2 * C, 0:DK]], 1)
        w = jnp.concatenate([uw[0:C, DK:W2], uw[C:2 * C, DK:W2]], 1)
        c["wq"] = jnp.concatenate([w.astype(BF16), c["qg"]], 0)
    for c in ch:
        c["s"] = s_ref[c["si"]]
        c["wqs"] = _bdot(c["wq"], c["s"].astype(BF16))
    for c in ch:
        c["vn"] = c["u"] - c["wqs"][0:C]
    for c in ch:
        o = c["wqs"][C:2 * C] + _bdot(c["attn"], bdw(c["vn"]))
        o_ref = (o0_ref, o1_ref)[c["d"]]
        o_ref[c["rows"], c["p"] * W2:(c["p"] + 1) * W2] = o.astype(o_ref.dtype)
    for c in ch:
        kv = lax.dot_general(c["kd"], c["vn"].astype(BF16), tn, preferred_element_type=F32)
        s_ref[c["si"]] = c["s"] * c["gls"] + jnp.where(sdiag, kv, 0.0)


def _dn_group(i):
    b = i // DN_XG
    g = i % DN_XG
    return jnp.where(i < MX // DN_TT, ((b // DN_NB) * DN_XG + g) * DN_NB + b % DN_NB, i)


def _dnscan(qkv, gb):
    cblk = MX // (DN_NB * DN_TT)
    rows = DN_NB * DN_TT

    def idx0(bp, s):
        return jnp.where(s == 0, cblk + bp, bp * DN_XG + s - 1)

    def idx1(bp, s):
        return jnp.where(s == 0, cblk + bp, bp * DN_XG + DN_XG - s)

    n_state = DN_NB * 2 * (H_DN // 2)
    est = 2 * 2 * (rows * 1536 * 2 + rows * 128 * 4 + rows * 512 * 4) + n_state * 4 * DK * DK * 4 + (4 << 20)
    return pl.pallas_call(
        _dnscan_kernel,
        grid=(B // DN_NB, 1 + DN_XG),
        in_specs=[
            pl.BlockSpec((rows, 3 * DN_W), lambda bp, s: (idx0(bp, s), 0)),
            pl.BlockSpec((rows, 3 * DN_W), lambda bp, s: (idx1(bp, s), 0)),
            pl.BlockSpec((rows, 128), lambda bp, s: (idx0(bp, s), 0)),
            pl.BlockSpec((rows, 128), lambda bp, s: (idx1(bp, s), 0)),
        ],
        out_specs=[
            pl.BlockSpec((rows, DN_W), lambda bp, s: (idx0(bp, s), 0)),
            pl.BlockSpec((rows, DN_W), lambda bp, s: (idx1(bp, s), 0)),
        ],
        out_shape=[jax.ShapeDtypeStruct((M, DN_W), BF16), jax.ShapeDtypeStruct((M, DN_W), BF16)],
        scratch_shapes=[pltpu.VMEM((n_state, 2 * DK, 2 * DK), F32)],
        compiler_params=_cparams(("arbitrary", "arbitrary"), est),
        name="dn_scan",
    )(qkv, qkv, gb, gb)


def _mlaproj_kernel(cq_ref, ckv_ref, sm_ref, cos_ref, sin_ref, gq_ref, gkv_ref,
                    wq1_ref, wq2_ref, wk_ref, wv_ref, q_ref, k_ref, v_ref):
    cq = cq_ref[...].astype(F32)
    hq = (cq * lax.rsqrt(jnp.sum(cq * cq, -1, keepdims=True) * (1.0 / Q_RANK) + EPS) * gq_ref[...]).astype(BF16)
    ckv = ckv_ref[...].astype(F32)
    hkv = (ckv * lax.rsqrt(jnp.mean(ckv * ckv, -1, keepdims=True) + EPS) * gkv_ref[...]).astype(BF16)
    q1 = _bdot(hq, wq1_ref[...])
    q2 = _bdot(hq, wq2_ref[...])
    kn = _bdot(hkv, wk_ref[...])
    vv = _bdot(hkv, wv_ref[...])
    cosq = cos_ref[...]
    sin = sin_ref[...]
    lane = lax.broadcasted_iota(jnp.int32, cosq.shape, 1)
    cosk = jnp.where(lane >= NOPE, cosq, 0.0)
    kpe = sm_ref[:, 0:HP] * cosk + sm_ref[:, HP:2 * HP] * sin
    for h in range(H_MLA):
        sl = slice(h * HP, (h + 1) * HP)
        q_ref[:, sl] = ((q1[:, sl] * cosq + q2[:, sl] * sin) * QK_SCALE).astype(BF16)
        k_ref[:, sl] = (kn[:, sl] + kpe).astype(BF16)
        v_ref[:, sl] = jnp.where(lane == V_DIM, 1.0, vv[:, sl]).astype(BF16)


def _mlaproj(u, usm, cos_t, sin_t, gq, gkv, wq1, wq2, wk, wv):
    tm = 512
    wide = H_MLA * HP
    est = 2 * (tm * (CQ_PAD + KV_RANK) * 2 + tm * 256 * 4 + 2 * tm * 128 * 4 + 2 * CQ_PAD * wide * 2
               + 2 * KV_RANK * wide * 2 + 3 * tm * wide * 2) + 6 * tm * wide * 4
    full = lambda shape: pl.BlockSpec(shape, lambda i: (0, 0))
    return pl.pallas_call(
        _mlaproj_kernel,
        grid=(M // tm,),
        in_specs=[
            pl.BlockSpec((tm, CQ_PAD), lambda i: (i, C_CQ // CQ_PAD)),
            pl.BlockSpec((tm, KV_RANK), lambda i: (i, C_CKV // KV_RANK)),
            pl.BlockSpec((tm, N_SMALL), lambda i: (i, 0)),
            pl.BlockSpec((tm, HP), lambda i: (i, 0)),
            pl.BlockSpec((tm, HP), lambda i: (i, 0)),
            full((1, CQ_PAD)), full((1, KV_RANK)),
            full((CQ_PAD, wide)), full((CQ_PAD, wide)), full((KV_RANK, wide)), full((KV_RANK, wide)),
        ],
        out_specs=[pl.BlockSpec((tm, wide), lambda i: (i, 0))] * 3,
        out_shape=[jax.ShapeDtypeStruct((M, wide), BF16)] * 3,
        compiler_params=_cparams(("arbitrary",), est),
        name="mla_proj",
    )(u, u, usm, cos_t, sin_t, gq, gkv, wq1, wq2, wk, wv)


ATT_TQ = 256
ATT_TK = 512
ATT_HG = 4


def _attn_kernel(q_ref, kx_ref, vx_ref, kc_ref, vc_ref, o_ref):
    qi = pl.program_id(2)
    nt = (((1,), (1,)), ((), ()))
    heads = [slice(g * HP, (g + 1) * HP) for g in range(ATT_HG)]
    qs = [q_ref[:, sl] for sl in heads]

    def scores(g, k_ref, rows):
        return lax.dot_general(qs[g], k_ref[rows, heads[g]], nt, preferred_element_type=F32)

    def update(carry, s, v_blk):
        m, acc = carry
        m_new = jnp.maximum(m, jnp.max(s, -1, keepdims=True))
        p = jnp.exp((s - m_new).astype(BF16))
        return m_new, jnp.exp(m - m_new) * acc + _bdot(p, v_blk)

    def finish(carries):
        for g, (_, acc) in enumerate(carries):
            o_ref[:, heads[g]] = (acc / acc[:, V_DIM:V_DIM + 1]).astype(o_ref.dtype)

    init = (jnp.full((ATT_TQ, 1), -jnp.inf, F32), jnp.zeros((ATT_TQ, HP), F32))
    everything = slice(None)

    def ctx_block():
        return [update(init, scores(g, kc_ref, everything), vc_ref[:, heads[g]]) for g in range(ATT_HG)]

    @pl.when(qi < T // ATT_TQ)
    def _():
        nblk = T // ATT_TK
        s_next = [scores(g, kx_ref, slice(0, ATT_TK)) for g in range(ATT_HG)]
        carries = ctx_block()
        for j in range(nblk):
            s = s_next
            if j + 1 < nblk:
                s_next = [scores(g, kx_ref, slice((j + 1) * ATT_TK, (j + 2) * ATT_TK)) for g in range(ATT_HG)]
            rows = slice(j * ATT_TK, (j + 1) * ATT_TK)
            carries = [update(carries[g], s[g], vx_ref[rows, heads[g]]) for g in range(ATT_HG)]
        finish(carries)

    @pl.when(qi >= T // ATT_TQ)
    def _():
        finish(ctx_block())


def _attention(q, k, v, with_ctx_queries):
    tq = ATT_TQ
    nqx = T // tq
    nq = nqx + (1 if with_ctx_queries else 0)
    cblk = MX // CTX
    rows = M if with_ctx_queries else MX

    def qidx(b, h, i):
        return (jnp.where(i < nqx, b * nqx + i, MX // tq + b), h)

    wide = ATT_HG * HP
    est = 2 * (2 * T * wide * 2 + 2 * CTX * wide * 2 + 2 * tq * wide * 2) + 8 * ATT_HG * tq * ATT_TK * 4
    return pl.pallas_call(
        _attn_kernel,
        grid=(B, H_MLA // ATT_HG, nq),
        in_specs=[
            pl.BlockSpec((tq, wide), qidx),
            pl.BlockSpec((T, wide), lambda b, h, i: (b, h)),
            pl.BlockSpec((T, wide), lambda b, h, i: (b, h)),
            pl.BlockSpec((CTX, wide), lambda b, h, i: (cblk + b, h)),
            pl.BlockSpec((CTX, wide), lambda b, h, i: (cblk + b, h)),
        ],
        out_specs=pl.BlockSpec((tq, wide), qidx),
        out_shape=jax.ShapeDtypeStruct((rows, H_MLA * HP), BF16),
        compiler_params=_cparams(("arbitrary", "arbitrary", "arbitrary"), est),
        name="mla_attention",
    )(q, k, v, k, v)


def _mm1_kernel(a_ref, b_ref, o_ref):
    o_ref[...] = _bdot(a_ref[...], b_ref[...]).astype(o_ref.dtype)


def _fourier_channels(u, w_ab):
    tm = 512
    est = 2 * (tm * F_W * 2 + F_W * 2 * F_W * 2 + tm * 2 * F_W * 2)
    return pl.pallas_call(
        _mm1_kernel,
        grid=(M // tm,),
        in_specs=[pl.BlockSpec((tm, F_W), lambda i: (i, C_F // F_W)),
                  pl.BlockSpec((F_W, 2 * F_W), lambda i: (0, 0))],
        out_specs=pl.BlockSpec((tm, 2 * F_W), lambda i: (i, 0)),
        out_shape=jax.ShapeDtypeStruct((M, 2 * F_W), BF16),
        compiler_params=_cparams(("arbitrary",), est),
        name="fourier_channels",
    )(u, w_ab)


def _dft_kernel(cs_ref, ab_ref, o_ref, acc_ref, *, nk):
    kk = pl.program_id(2)

    @pl.when(kk == 0)
    def _():
        acc_ref[...] = jnp.zeros_like(acc_ref)

    acc_ref[...] += _bdot(cs_ref[...], ab_ref[...])

    @pl.when(kk == nk - 1)
    def _():
        o_ref[...] = acc_ref[...].astype(o_ref.dtype)


def _fourier_tokens(cs, ab, seq, row0, tm, tk):
    nk = 2 * seq // tk
    kpb = seq // tk
    est = 2 * (tm * tk * 2 + tk * F_W * 2 + tm * F_W * 2) + tm * F_W * 4
    return pl.pallas_call(
        functools.partial(_dft_kernel, nk=nk),
        grid=(B, seq // tm, nk),
        in_specs=[
            pl.BlockSpec((tm, tk), lambda b, i, kk: (i, kk)),
            pl.BlockSpec((tk, F_W), lambda b, i, kk: (row0 // tk + b * kpb + kk % kpb, kk // kpb)),
        ],
        out_specs=pl.BlockSpec((tm, F_W), lambda b, i, kk: (b * (seq // tm) + i, 0)),
        out_shape=jax.ShapeDtypeStruct((B * seq, F_W), BF16),
        scratch_shapes=[pltpu.VMEM((tm, F_W), F32)],
        compiler_params=_cparams(("arbitrary", "arbitrary", "arbitrary"), est),
        name="fourier_tokens",
    )(cs, ab)


def _merge_kernel(x_ref, mod_ref, o0_ref, o1_ref, z_ref, yb_ref, ycx_ref, ycc_ref, gt_ref, gdn_ref,
                  wa_ref, wb_ref, wc_ref, wo_ref, out_ref, *, x_tiles):
    yc = jnp.where(pl.program_id(0) < x_tiles, ycx_ref[...], ycc_ref[...])
    o = o0_ref[...].astype(F32) + o1_ref[...].astype(F32)
    z = z_ref[...].astype(F32)
    gdn = gdn_ref[...]
    ya = []
    for h in range(H_DN):
        sl = slice(h * DK, (h + 1) * DK)
        oh = o[:, sl]
        zh = z[:, sl]
        yh = oh * lax.rsqrt(jnp.mean(oh * oh, -1, keepdims=True) + EPS) * gdn
        ya.append((yh * (zh * jax.nn.sigmoid(zh))).astype(BF16))
    ya = jnp.concatenate(ya, -1)
    s = jax.nn.sigmoid(gt_ref[:, 0:D].astype(F32)) * _bdot(ya, wa_ref[...])
    s = s + jax.nn.sigmoid(gt_ref[:, D:2 * D].astype(F32)) * _bdot(yb_ref[...], wb_ref[...])
    s = s + jax.nn.sigmoid(gt_ref[:, 2 * D:3 * D].astype(F32)) * _bdot(yc, wc_ref[...])
    mix = _bdot(s.astype(BF16), wo_ref[...])
    out_ref[...] = x_ref[...] + mod_ref[2:3, :] * mix


def _merge(X, mod, o0, o1, u, yb, ycx, ycc, gdn, wa, wb, wc, wo, rows):
    tm = 256
    x_tiles = MX // tm
    full = lambda shape: pl.BlockSpec(shape, lambda i: (0, 0))
    est = 2 * (2 * tm * D * 4 + 2 * tm * DN_W * 4 + tm * DN_W * 2 * 3 + tm * D * 2 + tm * 3 * D * 2
               + (2 * DN_W + 2 * D) * D * 2) + 6 * tm * D * 4
    return pl.pallas_call(
        functools.partial(_merge_kernel, x_tiles=x_tiles),
        grid=(rows // tm,),
        in_specs=[
            pl.BlockSpec((tm, D), lambda i: (i, 0)),
            _mod_spec(tm),
            pl.BlockSpec((tm, DN_W), lambda i: (_dn_group(i), 0)),
            pl.BlockSpec((tm, DN_W), lambda i: (_dn_group(i), 0)),
            pl.BlockSpec((tm, DN_W), lambda i: (i, C_Z // DN_W)),
            pl.BlockSpec((tm, H_MLA * HP), lambda i: (i, 0)),
            pl.BlockSpec((tm, F_W), lambda i: (jnp.minimum(i, x_tiles - 1), 0)),
            pl.BlockSpec((tm, F_W), lambda i: (jnp.maximum(i - x_tiles, 0), 0)),
            pl.BlockSpec((tm, 3 * D), lambda i: (i, 0)),
            full((1, DK)),
            full((DN_W, D)), full((H_MLA * HP, D)), full((F_W, D)), full((D, D)),
        ],
        out_specs=pl.BlockSpec((tm, D), lambda i: (i, 0)),
        out_shape=jax.ShapeDtypeStruct((rows, D), F32),
        compiler_params=_cparams(("arbitrary",), est),
        name="merge",
    )(X, mod, o0, o1, u, yb, ycx, ycc, u, gdn, wa, wb, wc, wo)


def _ffn_kernel(x_ref, g_ref, mod_ref, wg_ref, wu_ref, wd_ref, o_ref, h_ref, acc_ref, *, nf):
    f = pl.program_id(1)

    @pl.when(f == 0)
    def _():
        h_ref[...] = _norm_mod(x_ref[...], g_ref[...], mod_ref[3:4, :], mod_ref[4:5, :]).astype(BF16)
        acc_ref[...] = jnp.zeros_like(acc_ref)

    h = h_ref[...]
    a = _bdot(h, wg_ref[...])
    up = _bdot(h, wu_ref[...])
    act = (a * jax.nn.sigmoid(a) * up).astype(BF16)
    acc_ref[...] += _bdot(act, wd_ref[...])

    @pl.when(f == nf - 1)
    def _():
        o_ref[...] = x_ref[...] + mod_ref[5:6, :] * acc_ref[...]


def _ffn(X, gain, mod, wg, wu, wd):
    tm, tf = 512, 1408
    nf = D_FF // tf
    est = 2 * (2 * tm * D * 4 + 3 * D * tf * 2) + tm * D * 6 + 3 * tm * tf * 4
    return pl.pallas_call(
        functools.partial(_ffn_kernel, nf=nf),
        grid=(M // tm, nf),
        in_specs=[
            pl.BlockSpec((tm, D), lambda i, f: (i, 0)),
            pl.BlockSpec((1, D), lambda i, f: (0, 0)),
            _mod_spec(tm),
            pl.BlockSpec((D, tf), lambda i, f: (0, f)),
            pl.BlockSpec((D, tf), lambda i, f: (0, f)),
            pl.BlockSpec((tf, D), lambda i, f: (f, 0)),
        ],
        out_specs=pl.BlockSpec((tm, D), lambda i, f: (i, 0)),
        out_shape=jax.ShapeDtypeStruct((M, D), F32),
        scratch_shapes=[pltpu.VMEM((tm, D), BF16), pltpu.VMEM((tm, D), F32)],
        compiler_params=_cparams(("arbitrary", "arbitrary"), est),
        name="ffn_dense",
    )(X, gain, mod, wg, wu, wd)


def _router_kernel(x_ref, g_ref, mod_ref, wr_ref, h_ref, r_ref):
    h = _norm_mod(x_ref[...], g_ref[...], mod_ref[3:4, :], mod_ref[4:5, :])
    h_ref[...] = h
    h_hi = h.astype(BF16)
    h_lo = (h - h_hi.astype(F32)).astype(BF16)
    wr = wr_ref[...]
    w_hi = wr.astype(BF16)
    w_lo = (wr - w_hi.astype(F32)).astype(BF16)
    logits = _bdot(h_hi, w_hi) + _bdot(h_lo, w_hi) + _bdot(h_hi, w_lo)
    lane = lax.broadcasted_iota(jnp.int32, logits.shape, 1)
    logits = jnp.where(lane < N_EXP, logits, -jnp.inf)
    m1 = jnp.max(logits, -1, keepdims=True)
    i1 = jnp.min(jnp.where(logits == m1, lane, 128), -1, keepdims=True)
    rest = jnp.where(lane == i1, -jnp.inf, logits)
    m2 = jnp.max(rest, -1, keepdims=True)
    i2 = jnp.min(jnp.where(rest == m2, lane, 128), -1, keepdims=True)
    e2 = jnp.exp(m2 - m1)
    w1 = 1.0 / (1.0 + e2)
    w2 = e2 / (1.0 + e2)
    r = jnp.where(lane == 0, i1.astype(F32), 0.0)
    r = jnp.where(lane == 1, i2.astype(F32), r)
    r = jnp.where(lane == 2, w1, r)
    r = jnp.where(lane == 3, w2, r)
    r_ref[...] = r


def _router(X, gain, mod, wr_pad):
    tm = 512
    est = 2 * (tm * D * 4 + D * 128 * 4 + tm * D * 2 + tm * 128 * 4) + 4 * tm * D * 4
    return pl.pallas_call(
        _router_kernel,
        grid=(MX // tm,),
        in_specs=[
            pl.BlockSpec((tm, D), lambda i: (i, 0)),
            pl.BlockSpec((1, D), lambda i: (0, 0)),
            _mod_spec(tm),
            pl.BlockSpec((D, 128), lambda i: (0, 0)),
        ],
        out_specs=[pl.BlockSpec((tm, D), lambda i: (i, 0)), pl.BlockSpec((tm, 128), lambda i: (i, 0))],
        out_shape=[jax.ShapeDtypeStruct((MX, D), F32), jax.ShapeDtypeStruct((MX, 128), F32)],
        compiler_params=_cparams(("arbitrary",), est),
        name="moe_router",
    )(X, gain, mod, wr_pad)


MOE_TM = 512
MOE_TF = 1792
MOE_ROWS = MX * TOP_K + N_EXP * MOE_TM
MOE_TILES = MOE_ROWS // MOE_TM


def _moe_kernel(te_ref, nt_ref, tok_ref, h_hbm, rw_ref, wg_ref, wu_ref, wd_ref, o_ref,
                hbuf, sem, hb_ref, acc_ref, *, nf):
    t = pl.program_id(0)
    f = pl.program_id(1)
    n_tiles = nt_ref[0]
    valid = t < n_tiles

    def row_copy(tile, r, slot):
        return pltpu.make_async_copy(
            h_hbm.at[pl.ds(tok_ref[tile * MOE_TM + r], 1)], hbuf.at[slot, pl.ds(r, 1)], sem.at[slot])

    def start_tile(tile, slot):
        def body(r, carry):
            row_copy(tile, r, slot).start()
            return carry
        lax.fori_loop(0, MOE_TM, body, 0, unroll=8)

    def wait_tile(tile, slot):
        def body(r, carry):
            row_copy(tile, r, slot).wait()
            return carry
        lax.fori_loop(0, MOE_TM, body, 0, unroll=8)

    @pl.when(jnp.logical_and(valid, jnp.logical_and(f == 0, t == 0)))
    def _():
        start_tile(0, 0)

    @pl.when(jnp.logical_and(valid, f == 0))
    def _():
        slot = t % 2

        @pl.when(t + 1 < n_tiles)
        def _():
            start_tile(t + 1, 1 - slot)

        wait_tile(t, slot)
        hb_ref[...] = hbuf[slot].astype(BF16)
        acc_ref[...] = jnp.zeros_like(acc_ref)

    @pl.when(valid)
    def _():
        h = hb_ref[...]
        a = _bdot(h, wg_ref[...])
        up = _bdot(h, wu_ref[...])
        act = (a * jax.nn.sigmoid(a) * up).astype(BF16)
        acc_ref[...] += _bdot(act, wd_ref[...])

    @pl.when(jnp.logical_and(valid, f == nf - 1))
    def _():
        o_ref[...] = (acc_ref[...] * rw_ref[...]).astype(o_ref.dtype)

    @pl.when(jnp.logical_and(jnp.logical_not(valid), f == nf - 1))
    def _():
        o_ref[...] = jnp.zeros_like(o_ref)


def _moe_experts(tile_expert, n_tiles, row_token, h, row_w, wg, wu, wd):
    tm, tf = MOE_TM, MOE_TF
    nf = E_FF // tf

    def fsel(t, f, nt):
        return jnp.where(t < nt[0], f, nf - 1)

    est = 2 * (tm * 128 * 4 + 3 * D * tf * 2 + tm * D * 2) + 2 * tm * D * 4 + tm * D * 6 + 3 * tm * tf * 4
    grid_spec = pltpu.PrefetchScalarGridSpec(
        num_scalar_prefetch=3,
        grid=(MOE_TILES, nf),
        in_specs=[
            pl.BlockSpec(memory_space=pl.ANY),
            pl.BlockSpec((tm, 1), lambda t, f, te, nt, tok: (t, 0)),
            pl.BlockSpec((None, D, tf), lambda t, f, te, nt, tok: (te[t], 0, fsel(t, f, nt))),
            pl.BlockSpec((None, D, tf), lambda t, f, te, nt, tok: (te[t], 0, fsel(t, f, nt))),
            pl.BlockSpec((None, tf, D), lambda t, f, te, nt, tok: (te[t], fsel(t, f, nt), 0)),
        ],
        out_specs=pl.BlockSpec((tm, D), lambda t, f, te, nt, tok: (t, 0)),
        scratch_shapes=[pltpu.VMEM((2, tm, D), F32), pltpu.SemaphoreType.DMA((2,)),
                        pltpu.VMEM((tm, D), BF16), pltpu.VMEM((tm, D), F32)],
    )
    return pl.pallas_call(
        functools.partial(_moe_kernel, nf=nf),
        grid_spec=grid_spec,
        out_shape=jax.ShapeDtypeStruct((MOE_ROWS, D), BF16),
        compiler_params=_cparams(("arbitrary", "arbitrary"), est),
        name="moe_experts",
    )(tile_expert, n_tiles, row_token, h, row_w, wg, wu, wd)


def _final_kernel(x_ref, mod_ref, y0_ref, y1_ref, g_ref, o_ref):
    x = x_ref[...] + mod_ref[5:6, :] * (y0_ref[...].astype(F32) + y1_ref[...].astype(F32))
    o_ref[...] = x * lax.rsqrt(jnp.mean(x * x, -1, keepdims=True) + EPS) * g_ref[...]


def _final(X, mod, y0, y1, gain):
    tm = 512
    est = 2 * (2 * tm * D * 4 + 2 * tm * D * 2) + 3 * tm * D * 4
    return pl.pallas_call(
        _final_kernel,
        grid=(MX // tm,),
        in_specs=[
            pl.BlockSpec((tm, D), lambda i: (i, 0)),
            _mod_spec(tm),
            pl.BlockSpec((tm, D), lambda i: (i, 0)),
            pl.BlockSpec((tm, D), lambda i: (i, 0)),
            pl.BlockSpec((1, D), lambda i: (0, 0)),
        ],
        out_specs=pl.BlockSpec((tm, D), lambda i: (i, 0)),
        out_shape=jax.ShapeDtypeStruct((MX, D), F32),
        compiler_params=_cparams(("arbitrary",), est),
        name="final_norm",
    )(X, mod, y0, y1, gain)


def _moe_routing(route):
    tm = MOE_TM
    e_flat = route[:, 0:TOP_K].astype(jnp.int32).reshape(-1)
    w_flat = route[:, TOP_K:2 * TOP_K].reshape(-1)
    n = e_flat.shape[0]
    eids = jnp.arange(N_EXP, dtype=jnp.int32)[None, :]

    def pick(onehot, table):
        return jnp.sum(onehot * table[None, :], 1)

    order = jnp.argsort(e_flat, stable=True).astype(jnp.int32)
    inv_order = jnp.argsort(order).astype(jnp.int32)
    oh_f = (e_flat[:, None] == eids).astype(jnp.int32)
    counts = jnp.sum(oh_f, 0)
    padded = ((counts + tm - 1) // tm) * tm
    start = jnp.cumsum(counts) - counts
    pend = jnp.cumsum(padded)
    pstart = pend - padded
    slot_row = pick(oh_f, pstart - start) + inv_order

    n_tiles = (pend[-1] // tm).astype(jnp.int32)
    tile_start = jnp.arange(MOE_TILES, dtype=jnp.int32) * tm
    tile_expert = jnp.minimum(jnp.sum(tile_start[:, None] >= pend[None, :], 1), N_EXP - 1).astype(jnp.int32)
    rows = jnp.arange(MOE_ROWS, dtype=jnp.int32)
    oh_r = (jnp.repeat(tile_expert, tm)[:, None] == eids).astype(jnp.int32)
    rank_r = rows - pick(oh_r, pstart)
    valid = jnp.logical_and(rank_r < pick(oh_r, counts), rows < pend[-1])
    slot_r = order[jnp.clip(pick(oh_r, start) + rank_r, 0, n - 1)]
    row_token = jnp.where(valid, slot_r // TOP_K, 0)
    row_w = jnp.where(valid, w_flat[slot_r], 0.0)
    last_expert = jnp.sum(jnp.where(jnp.arange(MOE_TILES) == n_tiles - 1, tile_expert, 0))
    tile_expert = jnp.where(tile_start < pend[-1], tile_expert, last_expert)
    return row_token, row_w.reshape(MOE_ROWS, 1), slot_row.reshape(MX, TOP_K), tile_expert, n_tiles.reshape(1)


def _dft_table(n):
    split = 64 if n > 64 else 1
    t = jnp.arange(n, dtype=jnp.int32)[None, :]
    a = jnp.arange(n // split, dtype=jnp.int32)[:, None]
    b = jnp.arange(split, dtype=jnp.int32)[:, None]
    na = n // split
    ang_a = ((a * t) % na).astype(F32) * (2.0 * np.pi / na)
    ang_b = ((b * t) % n).astype(F32) * (2.0 * np.pi / n)
    scale = n ** -0.5
    ca, sa = jnp.cos(ang_a)[:, None, :], jnp.sin(ang_a)[:, None, :]
    cb, sb = (jnp.cos(ang_b) * scale)[None, :, :], (jnp.sin(ang_b) * scale)[None, :, :]
    c = (ca * cb - sa * sb).reshape(n, n)
    s = (sa * cb + ca * sb).reshape(n, n)
    return jnp.concatenate([c, s], 1).astype(BF16)


def _channel_table():
    k = jnp.arange(F_GDIM, dtype=jnp.int32)
    ang = ((k[:, None] * k[None, :]) % F_GDIM).astype(F32) * (2.0 * np.pi / F_GDIM)
    scale = F_GDIM ** -0.5
    eye = jnp.eye(F_GROUPS, dtype=F32)
    return jnp.concatenate([jnp.kron(eye, jnp.cos(ang) * scale), jnp.kron(eye, -jnp.sin(ang) * scale)], 1).astype(BF16)


def _rope_tables():
    t = jnp.arange(T, dtype=jnp.int32)
    r = (t // GRID_W).astype(F32)[:, None]
    c = (t % GRID_W).astype(F32)[:, None]
    axis_dim = ROPE // 2
    inv = ROPE_BASE ** (-jnp.arange(0, axis_dim, 2, dtype=F32) / axis_dim)[None, :]
    cr, sr, cc, sc = jnp.cos(r * inv), jnp.sin(r * inv), jnp.cos(c * inv), jnp.sin(c * inv)
    cos32 = jnp.concatenate([cr, cr, cc, cc], 1)
    sin32 = jnp.concatenate([-sr, sr, -sc, sc], 1)
    ones = jnp.ones((T, NOPE), F32)
    zeros = jnp.zeros((T, NOPE), F32)
    tail = jnp.zeros((T, HP - NOPE - ROPE), F32)
    cos_x = jnp.concatenate([ones, cos32, tail], 1)
    sin_x = jnp.concatenate([zeros, sin32, tail], 1)
    cos_c = jnp.concatenate([jnp.ones((MC, NOPE + ROPE), F32), jnp.zeros((MC, HP - NOPE - ROPE), F32)], 1)
    cos_t = jnp.concatenate([jnp.tile(cos_x, (B, 1)), cos_c], 0)
    sin_t = jnp.concatenate([jnp.tile(sin_x, (B, 1)), jnp.zeros((MC, HP), F32)], 0)
    return cos_t, sin_t


_ROPE_SWAP = np.concatenate([np.arange(8, 16), np.arange(0, 8), np.arange(24, 32), np.arange(16, 24)])


def _layer_weights(w_in, w_qup, w_kvup, w_branch):
    sizes = (DN_W, DN_W, DN_W, DN_W, 2 * H_DN, 2 * H_DN, Q_RANK, KV_RANK, ROPE, F_W, 3 * D)
    offs = np.concatenate([[0], np.cumsum(sizes)])
    seg = [w_in[:, offs[i]:offs[i + 1]] for i in range(len(sizes))]
    dq, dk, dv, dz, da, db, cq, ckv, kpe, fo, gates = seg
    zc = lambda n: jnp.zeros((D, n), F32)
    w_main = jnp.concatenate([gates, dq, dk, dv, dz, fo, cq, zc(CQ_PAD - Q_RANK), ckv], 1).astype(BF16)
    w_small = jnp.concatenate([da, db, zc(NOPE - 4 * H_DN), kpe, zc(HP - NOPE - ROPE),
                               zc(NOPE), kpe[:, _ROPE_SWAP], zc(HP - NOPE - ROPE)], 1).astype(BF16)

    qh = w_qup.reshape(Q_RANK, H_MLA, NOPE + ROPE)
    zq = lambda n: jnp.zeros((Q_RANK, H_MLA, n), F32)
    wq1 = jnp.concatenate([qh, zq(HP - NOPE - ROPE)], 2).reshape(Q_RANK, H_MLA * HP)
    wq2 = jnp.concatenate([zq(NOPE), qh[:, :, NOPE:][:, :, _ROPE_SWAP], zq(HP - NOPE - ROPE)], 2).reshape(Q_RANK, H_MLA * HP)
    rowpad = jnp.zeros((CQ_PAD - Q_RANK, H_MLA * HP), F32)
    wq1 = jnp.concatenate([wq1, rowpad], 0).astype(BF16)
    wq2 = jnp.concatenate([wq2, rowpad], 0).astype(BF16)
    kvh = w_kvup.reshape(KV_RANK, H_MLA, NOPE + V_DIM)
    zk = jnp.zeros((KV_RANK, H_MLA, HP - NOPE), F32)
    wk = jnp.concatenate([kvh[:, :, :NOPE], zk], 2).reshape(KV_RANK, H_MLA * HP).astype(BF16)
    wv = jnp.concatenate([kvh[:, :, NOPE:], zk], 2).reshape(KV_RANK, H_MLA * HP).astype(BF16)

    wb = w_branch[1].reshape(H_MLA, V_DIM, D)
    wb = jnp.concatenate([wb, jnp.zeros((H_MLA, HP - V_DIM, D), F32)], 1).reshape(H_MLA * HP, D)
    return dict(w_main=w_main, w_small=w_small, wq1=wq1, wq2=wq2, wk=wk, wv=wv,
                wa=w_branch[0].astype(BF16), wb=wb.astype(BF16), wc=w_branch[2].astype(BF16))


def kernel(x, c, ctx, c_ctx, w_mod, b_mod, norm_mix, norm_ffn, w_in, dn_conv, dn_a_log, dn_dt_bias, dn_norm, mla_q_norm, mla_kv_norm, mla_w_qup, mla_w_kvup, w_branch, w_out, ffn_w_gate, ffn_w_up, ffn_w_down, moe_router, moe_w_gate, moe_w_up, moe_w_down, final_norm):
    X = jnp.concatenate([x.reshape(MX, D), ctx.reshape(MC, D)], 0)
    mods = _mod_vectors(c, c_ctx, w_mod, b_mod)
    cos_t, sin_t = _rope_tables()
    cs_x = _dft_table(T)
    cs_c = _dft_table(CTX)
    w_ab = _channel_table()

    out = None
    for l in range(DEPTH):
        last = l == DEPTH - 1
        mod = mods[l]
        lw = _layer_weights(w_in[l], mla_w_qup[l], mla_w_kvup[l], w_branch[l])
        u, usm = _inproj(X, norm_mix[l].reshape(1, D), mod, lw["w_main"], lw["w_small"])
        qkv, gb = _dnpre(u, usm, dn_conv[l], dn_a_log[l], dn_dt_bias[l])
        o0, o1 = _dnscan(qkv, gb)
        gq = jnp.concatenate([mla_q_norm[l], jnp.zeros((CQ_PAD - Q_RANK,), F32)]).reshape(1, CQ_PAD)
        q, k, v = _mlaproj(u, usm, cos_t, sin_t, gq, mla_kv_norm[l].reshape(1, KV_RANK),
                           lw["wq1"], lw["wq2"], lw["wk"], lw["wv"])
        yb = _attention(q, k, v, with_ctx_queries=not last)
        ab = _fourier_channels(u, w_ab)
        ycx = _fourier_tokens(cs_x, ab, T, 0, 1024, 2048)
        ycc = ycx if last else _fourier_tokens(cs_c, ab, CTX, MX, CTX, CTX)
        rows = MX if last else M
        X = _merge(X, mod, o0, o1, u, yb, ycx, ycc, dn_norm[l].reshape(1, DK),
                   lw["wa"], lw["wb"], lw["wc"], w_out[l].astype(BF16), rows)
        e = l // 2
        if l % 2 == 0:
            X = _ffn(X, norm_ffn[l].reshape(1, D), mod, ffn_w_gate[e].astype(BF16),
                     ffn_w_up[e].astype(BF16), ffn_w_down[e].astype(BF16))
        else:
            wr = jnp.concatenate([moe_router[e], jnp.zeros((D, 128 - N_EXP), F32)], 1)
            h, route = _router(X, norm_ffn[l].reshape(1, D), mod, wr)
            row_token, row_w, slot_row, tile_expert, n_tiles = _moe_routing(route)
            y = _moe_experts(tile_expert, n_tiles, row_token, h, row_w, moe_w_gate[e].astype(BF16),
                             moe_w_up[e].astype(BF16), moe_w_down[e].astype(BF16))
            y0 = jnp.take(y, slot_row[:, 0], axis=0)
            y1 = jnp.take(y, slot_row[:, 1], axis=0)
            out = _final(X, mod, y0, y1, final_norm.reshape(1, D))
    return out.reshape(B, T, D)
```
